```python
import math, functools
import jax, jax.numpy as jnp
from jax import lax
import numpy as np

D_MODEL = 1024
BATCH = 4
SEQ = 8192
DEPTH = 4
DEC_BATCH = 16
DEC_SEQ = 4096
PAST_LEN = 128

N_META = 16
N_MIXERS = 2
N_HYENA_LAYERS = (DEPTH + 1) // 2
N_ATTN_LAYERS = DEPTH // 2
HYENA_ORDER = 2
SHORT_CONV = 3
FILTER_EMB = 33
FILTER_HIDDEN = 64
FILTER_BANDS = (FILTER_EMB - 1) // 2
DECAY_TARGET = 1e-2
FAST_DECAY_PCT = 0.3
SLOW_DECAY_PCT = 1.5
FILTER_OUT_SCALE = 0.02
N_HEADS = 16
N_KV_HEADS = 4
HEAD_DIM = D_MODEL // N_HEADS
GROUP = N_HEADS // N_KV_HEADS
WINDOW = 128
BLOCK = 128
REL_BUCKETS = 32
REL_MAX_DIST = 128
D_FF = 4 * D_MODEL
EPS = 1e-6
NEG = -1e30

kernel_name = 'hybrid_hyena_swa_encoder'


def _rmsnorm(x, g):
    xf = x.astype(jnp.float32)
    xf = xf * lax.rsqrt(jnp.mean(xf * xf, axis=-1, keepdims=True) + EPS)
    return xf.astype(x.dtype) * g


def _short_conv(u, w, b):
    up = jnp.pad(u, ((0, 0), (1, 1), (0, 0)))
    return up[:, :-2] * w[0] + up[:, 1:-1] * w[1] + up[:, 2:] * w[2] + b


def _hyena_filters(L, f_w1, f_b1, f_freq1, f_w2, f_b2, f_freq2, f_w3):
    t = jnp.linspace(0.0, 1.0, L, dtype=jnp.float32)[:, None]
    w = (2.0 * math.pi / L) * jnp.arange(L, dtype=jnp.float32)[:, None]
    f = jnp.linspace(1e-4, FILTER_BANDS - 1, FILTER_BANDS, dtype=jnp.float32)[None, :]
    z = jnp.concatenate([t, jnp.cos(f * w), -jnp.sin(f * w)], axis=-1).astype(f_w1.dtype)
    h = jnp.sin(f_freq1 * (z @ f_w1 + f_b1))
    h = jnp.sin(f_freq2 * (h @ f_w2 + f_b2))
    k = (h @ f_w3).astype(jnp.float32).reshape(L, HYENA_ORDER, 2, D_MODEL)
    max_decay = math.log(DECAY_TARGET) / FAST_DECAY_PCT
    min_decay = math.log(DECAY_TARGET) / SLOW_DECAY_PCT
    deltas = jnp.abs(jnp.linspace(min_decay, max_decay, D_MODEL, dtype=jnp.float32))
    decay = jnp.exp(-t * deltas)
    return k * decay[:, None, None, :]


def _bidir_long_conv(u, k_fwd, k_bwd, skip):
    B, L, D = u.shape
    n_fft = 2 * L
    kern = jnp.concatenate([k_fwd, jnp.zeros((1, D), k_fwd.dtype), k_bwd[:0:-1]], axis=0)
    kf = jnp.fft.rfft(kern, n=n_fft, axis=0)
    uf32 = u.astype(jnp.float32)
    uf = jnp.fft.rfft(uf32, n=n_fft, axis=1)
    y = jnp.fft.irfft(uf * kf[None], n=n_fft, axis=1)[:, :L]
    return (y + uf32 * skip.astype(jnp.float32)).astype(u.dtype)


def _hyena_mixer(xn, w_in, conv_w, conv_b, f_w1, f_b1, f_freq1, f_w2, f_b2, f_freq2, f_w3, skip, w_out):
    L = xn.shape[1]
    u = _short_conv(xn @ w_in, conv_w, conv_b)
    x1, x2, v = jnp.split(u, 3, axis=-1)
    filt = _hyena_filters(L, f_w1, f_b1, f_freq1, f_w2, f_b2, f_freq2, f_w3)
    z = v
    for o, gate in enumerate((x1, x2)):
        z = gate * _bidir_long_conv(z, filt[:, o, 0], filt[:, o, 1], skip[o])
    return z @ w_out


def _t5_bucket(rel):
    half = REL_BUCKETS // 2
    max_exact = half // 2
    n = jnp.abs(rel)
    large = max_exact + (jnp.log(jnp.maximum(n, 1).astype(jnp.float32) / max_exact)
                         / math.log(REL_MAX_DIST / max_exact) * (half - max_exact)).astype(jnp.int32)
    large = jnp.minimum(large, half - 1)
    return jnp.where(rel > 0, half, 0) + jnp.where(n < max_exact, n, large)


def _swa_mixer(xn, w_qkv, q_gain, k_gain, sink, w_out, rel_bias):
    B, L, _ = xn.shape
    nb = -(-L // BLOCK)
    Lp = nb * BLOCK
    qkv = xn @ w_qkv
    q, k, v = jnp.split(qkv, [N_HEADS * HEAD_DIM, (N_HEADS + N_KV_HEADS) * HEAD_DIM], axis=-1)
    q = _rmsnorm(q.reshape(B, L, N_KV_HEADS, GROUP, HEAD_DIM), q_gain) * (HEAD_DIM ** -0.5)
    k = _rmsnorm(k.reshape(B, L, N_KV_HEADS, HEAD_DIM), k_gain)
    v = v.reshape(B, L, N_KV_HEADS, HEAD_DIM)

    q_pos = jnp.arange(Lp, dtype=jnp.int32).reshape(nb, BLOCK)
    k_pos = (jnp.arange(nb, dtype=jnp.int32) * BLOCK - BLOCK)[:, None] + jnp.arange(3 * BLOCK, dtype=jnp.int32)[None, :]
    kp3 = k_pos[:, None, :]
    band_ok = (jnp.abs(kp3 - q_pos[:, :, None]) <= WINDOW) & (kp3 >= N_META) & (kp3 < L)
    band_rel = jnp.arange(3 * BLOCK, dtype=jnp.int32)[None, :] - BLOCK - jnp.arange(BLOCK, dtype=jnp.int32)[:, None]
    band_bias = rel_bias[_t5_bucket(band_rel)].astype(jnp.float32).transpose(2, 0, 1).reshape(
        N_KV_HEADS, GROUP, BLOCK, 3 * BLOCK)
    meta_rel = jnp.arange(N_META, dtype=jnp.int32)[None, None, :] - q_pos[:, :, None]
    meta_bias = rel_bias[_t5_bucket(meta_rel)].astype(jnp.float32).transpose(0, 3, 1, 2).reshape(
        nb, N_KV_HEADS, GROUP, BLOCK, N_META)
    sink_logit = jnp.broadcast_to(sink.astype(jnp.float32).reshape(1, N_KV_HEADS, GROUP, 1, 1),
                                  (nb, N_KV_HEADS, GROUP, BLOCK, 1))

    def one_sequence(qkv_one):
        q1, k1, v1 = qkv_one
        qb = jnp.pad(q1, ((0, Lp - L), (0, 0), (0, 0), (0, 0))).reshape(nb, BLOCK, N_KV_HEADS, GROUP, HEAD_DIM)
        pad = ((BLOCK, Lp - L + BLOCK), (0, 0), (0, 0))
        kp = jnp.pad(k1, pad).reshape(nb + 2, BLOCK, N_KV_HEADS, HEAD_DIM)
        vp = jnp.pad(v1, pad).reshape(nb + 2, BLOCK, N_KV_HEADS, HEAD_DIM)
        kb = jnp.concatenate([kp[:-2], kp[1:-1], kp[2:]], axis=1)
        vb = jnp.concatenate([vp[:-2], vp[1:-1], vp[2:]], axis=1)
        km, vm = k1[:N_META], v1[:N_META]
        s_meta = jnp.einsum('nqhgd,mhd->nhgqm', qb, km).astype(jnp.float32) + meta_bias
        s_band = jnp.einsum('nqhgd,nshd->nhgqs', qb, kb).astype(jnp.float32) + band_bias
        s_band = jnp.where(band_ok[:, None, None], s_band, NEG)
        p = jax.nn.softmax(jnp.concatenate([s_meta, s_band, sink_logit], axis=-1), axis=-1).astype(v1.dtype)
        o = (jnp.einsum('nhgqm,mhd->nqhgd', p[..., :N_META], vm)
             + jnp.einsum('nhgqs,nshd->nqhgd', p[..., N_META:N_META + 3 * BLOCK], vb))
        return o.reshape(Lp, N_HEADS * HEAD_DIM)[:L]

    o = lax.map(one_sequence, (q, k, v))
    return o @ w_out


def _sqrelu_mlp(xn, w_up, w_down):
    return jnp.square(jax.nn.relu(xn @ w_up)) @ w_down


def _trunk(x, meta_tokens, rel_bias, mix_norm, mlp_norm,
           hy_w_in, hy_conv_w, hy_conv_b, hy_f_w1, hy_f_b1, hy_f_freq1, hy_f_w2, hy_f_b2, hy_f_freq2,
           hy_f_w3, hy_skip, hy_w_out, at_w_qkv, at_q_norm, at_k_norm, at_sink, at_w_out,
           mlp_w_up, mlp_w_down):
    B = x.shape[0]
    meta = jnp.broadcast_to(meta_tokens[None].astype(x.dtype), (B, N_META, D_MODEL))
    h = jnp.concatenate([meta, x], axis=1)
    for i in range(DEPTH):
        j = i // N_MIXERS
        hn = _rmsnorm(h, mix_norm[i])
        if i % N_MIXERS == 0:
            h = h + _hyena_mixer(hn, hy_w_in[j], hy_conv_w[j], hy_conv_b[j], hy_f_w1[j], hy_f_b1[j],
                                 hy_f_freq1[j], hy_f_w2[j], hy_f_b2[j], hy_f_freq2[j], hy_f_w3[j],
                                 hy_skip[j], hy_w_out[j])
        else:
            h = h + _swa_mixer(hn, at_w_qkv[j], at_q_norm[j], at_k_norm[j], at_sink[j], at_w_out[j], rel_bias)
        h = h + _sqrelu_mlp(_rmsnorm(h, mlp_norm[i]), mlp_w_up[i], mlp_w_down[i])
    return h[:, N_META:]


def setup_inputs(seed: int = 0) -> dict:
    key = jax.random.key(seed)
    ks = iter(jax.random.split(key, 32))

    def nrm(shape, scale):
        return scale * jax.random.normal(next(ks), shape, jnp.float32)

    def gain(shape):
        return 1.0 + 0.1 * jax.random.normal(next(ks), shape, jnp.float32)

    NH, NA, D = N_HYENA_LAYERS, N_ATTN_LAYERS, D_MODEL
    qkv_cols = (N_HEADS + 2 * N_KV_HEADS) * HEAD_DIM
    return {
        'x_prompt': nrm((BATCH, SEQ, D), 1.0),
        'x_sample': nrm((DEC_BATCH, DEC_SEQ, D), 1.0),
        'meta_tokens': nrm((N_META, D), 1.0),
        'rel_bias': nrm((REL_BUCKETS, N_HEADS), 0.5),
        'mix_norm': gain((DEPTH, D)),
        'mlp_norm': gain((DEPTH, D)),
        'hy_w_in': nrm((NH, D, 3 * D), D ** -0.5),
        'hy_conv_w': nrm((NH, SHORT_CONV, 3 * D), SHORT_CONV ** -0.5),
        'hy_conv_b': nrm((NH, 3 * D), 0.02),
        'hy_f_w1': nrm((NH, FILTER_EMB, FILTER_HIDDEN), FILTER_EMB ** -0.5),
        'hy_f_b1': nrm((NH, FILTER_HIDDEN), 0.1),
        'hy_f_freq1': gain((NH, FILTER_HIDDEN)),
        'hy_f_w2': nrm((NH, FILTER_HIDDEN, FILTER_HIDDEN), FILTER_HIDDEN ** -0.5),
        'hy_f_b2': nrm((NH, FILTER_HIDDEN), 0.1),
        'hy_f_freq2': gain((NH, FILTER_HIDDEN)),
        'hy_f_w3': nrm((NH, FILTER_HIDDEN, HYENA_ORDER * 2 * D), FILTER_OUT_SCALE * FILTER_HIDDEN ** -0.5),
        'hy_skip': nrm((NH, HYENA_ORDER, D), 0.1),
        'hy_w_out': nrm((NH, D, D), D ** -0.5),
        'at_w_qkv': nrm((NA, D, qkv_cols), D ** -0.5),
        'at_q_norm': gain((NA, HEAD_DIM)),
        'at_k_norm': gain((NA, HEAD_DIM)),
        'at_sink': nrm((NA, N_HEADS), 1.0),
        'at_w_out': nrm((NA, N_HEADS * HEAD_DIM, D), (N_HEADS * HEAD_DIM) ** -0.5),
        'mlp_w_up': nrm((DEPTH, D, D_FF), D ** -0.5),
        'mlp_w_down': nrm((DEPTH, D_FF, D), D_FF ** -0.5),
    }


def reference(x_prompt, x_sample, meta_tokens, rel_bias, mix_norm, mlp_norm,
              hy_w_in, hy_conv_w, hy_conv_b, hy_f_w1, hy_f_b1, hy_f_freq1, hy_f_w2, hy_f_b2, hy_f_freq2,
              hy_f_w3, hy_skip, hy_w_out, at_w_qkv, at_q_norm, at_k_norm, at_sink, at_w_out,
              mlp_w_up, mlp_w_down):
    trunk = functools.partial(
        _trunk, meta_tokens=meta_tokens, rel_bias=rel_bias, mix_norm=mix_norm, mlp_norm=mlp_norm,
        hy_w_in=hy_w_in, hy_conv_w=hy_conv_w, hy_conv_b=hy_conv_b, hy_f_w1=hy_f_w1, hy_f_b1=hy_f_b1,
        hy_f_freq1=hy_f_freq1, hy_f_w2=hy_f_w2, hy_f_b2=hy_f_b2, hy_f_freq2=hy_f_freq2, hy_f_w3=hy_f_w3,
        hy_skip=hy_skip, hy_w_out=hy_w_out, at_w_qkv=at_w_qkv, at_q_norm=at_q_norm, at_k_norm=at_k_norm,
        at_sink=at_sink, at_w_out=at_w_out, mlp_w_up=mlp_w_up, mlp_w_down=mlp_w_down)
    y_prompt = trunk(x_prompt)
    y_sample = trunk(x_sample)
    return (y_prompt, y_sample)
```

```python
import functools
import math

import jax
import jax.numpy as jnp
from jax import lax
from jax.experimental import pallas as pl
from jax.experimental.pallas import tpu as pltpu

F32 = jnp.float32
BF16 = jnp.bfloat16

D_MODEL = 1024
N_META = 16
HYENA_ORDER = 2
FILTER_EMB = 33
FILTER_HIDDEN = 64
FILTER_BANDS = (FILTER_EMB - 1) // 2
DECAY_TARGET = 1e-2
FAST_DECAY_PCT = 0.3
SLOW_DECAY_PCT = 1.5
N_HEADS = 16
N_KV_HEADS = 4
HEAD_DIM = D_MODEL // N_HEADS
GROUP = N_HEADS // N_KV_HEADS
WINDOW = 128
BLOCK = 128
REL_BUCKETS = 32
REL_MAX_DIST = 128
D_FF = 4 * D_MODEL
EPS = 1e-6
NEG = -1e30

PAGE = 128
FEAT_PAD = 128
VMEM_LIMIT_BYTES = 52 * 1024 * 1024
HALO = 16


def _cparams(*sem):
    return pltpu.CompilerParams(dimension_semantics=sem, vmem_limit_bytes=VMEM_LIMIT_BYTES)


def _rms(x, g):
    ms = jnp.mean(x * x, axis=-1, keepdims=True)
    return (x * lax.rsqrt(ms + EPS)) * g


def _norm_matmul_body(x_ref, g_ref, w_ref, o_ref, *, seq_len, tm):
    xn = _rms(x_ref[0], g_ref[...])
    y = jnp.dot(xn.astype(BF16), w_ref[...], preferred_element_type=F32)
    if seq_len is not None:
        row = pl.program_id(1) * tm + lax.broadcasted_iota(jnp.int32, (tm, 1), 0)
        y = jnp.where(row < seq_len, y, 0.0)
    o_ref[0] = y.astype(o_ref.dtype)


def _norm_matmul(h, g, w, *, tm, seq_len=None):
    b, lp, d = h.shape
    n = w.shape[1]
    return pl.pallas_call(
        functools.partial(_norm_matmul_body, seq_len=seq_len, tm=tm),
        grid=(b, lp // tm),
        in_specs=[
            pl.BlockSpec((1, tm, d), lambda i, j: (i, j, 0)),
            pl.BlockSpec((1, d), lambda i, j: (0, 0)),
            pl.BlockSpec((d, n), lambda i, j: (0, 0)),
        ],
        out_specs=pl.BlockSpec((1, tm, n), lambda i, j: (i, j, 0)),
        out_shape=jax.ShapeDtypeStruct((b, lp, n), BF16),
        compiler_params=_cparams("parallel", "parallel"),
        name="norm_matmul",
    )(h, g.reshape(1, d), w)


def _proj_residual_body(h_ref, z_ref, w_ref, o_ref):
    o_ref[...] = h_ref[...] + jnp.dot(z_ref[...], w_ref[...], preferred_element_type=F32)


def _proj_residual(h, z, w, *, tm):
    m, d = h.shape
    k = z.shape[1]
    return pl.pallas_call(
        _proj_residual_body,
        grid=(m // tm,),
        in_specs=[
            pl.BlockSpec((tm, d), lambda i: (i, 0)),
            pl.BlockSpec((tm, k), lambda i: (i, 0)),
            pl.BlockSpec((k, d), lambda i: (0, 0)),
        ],
        out_specs=pl.BlockSpec((tm, d), lambda i: (i, 0)),
        out_shape=jax.ShapeDtypeStruct((m, d), F32),
        compiler_params=_cparams("parallel"),
        name="proj_residual",
    )(h, z, w)


def _mlp_body(x_ref, g_ref, wu_ref, wd_ref, o_ref, *, f_chunk):
    x = x_ref[...]
    xb = _rms(x, g_ref[...]).astype(BF16)
    acc = x
    for f in range(D_FF // f_chunk):
        u = jnp.dot(xb, wu_ref[:, f * f_chunk:(f + 1) * f_chunk], preferred_element_type=F32)
        a = jnp.square(jnp.maximum(u, 0.0)).astype(BF16)
        acc = acc + jnp.dot(a, wd_ref[f * f_chunk:(f + 1) * f_chunk, :], preferred_element_type=F32)
    o_ref[...] = acc


def _mlp(h, g, w_up, w_down, *, tm, f_chunk=1024):
    m, d = h.shape
    return pl.pallas_call(
        functools.partial(_mlp_body, f_chunk=f_chunk),
        grid=(m // tm,),
        in_specs=[
            pl.BlockSpec((tm, d), lambda i: (i, 0)),
            pl.BlockSpec((1, d), lambda i: (0, 0)),
            pl.BlockSpec((d, D_FF), lambda i: (0, 0)),
            pl.BlockSpec((D_FF, d), lambda i: (0, 0)),
        ],
        out_specs=pl.BlockSpec((tm, d), lambda i: (i, 0)),
        out_shape=jax.ShapeDtypeStruct((m, d), F32),
        compiler_params=_cparams("parallel"),
        name="mlp",
    )(h, g.reshape(1, d), w_up, w_down)


def _short_conv_body(prev_ref, cur_ref, next_ref, w_ref, b_ref, o_ref, *, seq_len, tm):
    j = pl.program_id(1)
    x = cur_ref[0].astype(F32)
    prev_row = jnp.where(j > 0, prev_ref[0].astype(F32)[HALO - 1:HALO, :], 0.0)
    next_row = next_ref[0].astype(F32)[0:1, :]
    rid = lax.broadcasted_iota(jnp.int32, (tm, 1), 0)
    up = jnp.where(rid == 0, prev_row, pltpu.roll(x, 1, 0))
    dn = jnp.where(rid == tm - 1, next_row, pltpu.roll(x, tm - 1, 0))
    y = up * w_ref[0, 0:1, :] + x * w_ref[0, 1:2, :] + dn * w_ref[0, 2:3, :] + b_ref[0]
    y = jnp.where(j * tm + rid < seq_len, y, 0.0)
    o_ref[0, 0] = y.astype(o_ref.dtype)


def _short_conv(u, w, bias, *, tm, seq_len):
    b, lp, d3 = u.shape
    d = d3 // 3
    nh = tm // HALO
    last = lp // HALO - 1
    w3 = w.reshape(3, 3, d).transpose(1, 0, 2)
    b3 = bias.reshape(3, 1, d)
    return pl.pallas_call(
        functools.partial(_short_conv_body, seq_len=seq_len, tm=tm),
        grid=(b, lp // tm, 3),
        in_specs=[
            pl.BlockSpec((1, HALO, d), lambda i, j, c: (i, jnp.maximum(j * nh - 1, 0), c)),
            pl.BlockSpec((1, tm, d), lambda i, j, c: (i, j, c)),
            pl.BlockSpec((1, HALO, d), lambda i, j, c: (i, jnp.minimum((j + 1) * nh, last), c)),
            pl.BlockSpec((1, 3, d), lambda i, j, c: (c, 0, 0)),
            pl.BlockSpec((1, 1, d), lambda i, j, c: (c, 0, 0)),
        ],
        out_specs=pl.BlockSpec((1, 1, tm, d), lambda i, j, c: (c, i, j, 0)),
        out_shape=jax.ShapeDtypeStruct((3, b, lp, d), BF16),
        compiler_params=_cparams("parallel", "parallel", "parallel"),
        name="short_conv",
    )(u, u, u, w3, b3)


def _filter_body(z_ref, w1_ref, b1_ref, fr1_ref, w2_ref, b2_ref, fr2_ref, w3_ref, delta_ref, o_ref):
    hp = lax.Precision.HIGHEST
    z = z_ref[...]
    h = jnp.sin(fr1_ref[...] * (jnp.dot(z, w1_ref[...], precision=hp, preferred_element_type=F32) + b1_ref[...]))
    h = jnp.sin(fr2_ref[...] * (jnp.dot(h, w2_ref[...], precision=hp, preferred_element_type=F32) + b2_ref[...]))
    k = jnp.dot(h, w3_ref[...], precision=hp, preferred_element_type=F32)
    decay = jnp.exp(-z[:, 0:1] * delta_ref[...])
    d = delta_ref.shape[1]
    for q in range(2 * HYENA_ORDER):
        o_ref[:, q * d:(q + 1) * d] = k[:, q * d:(q + 1) * d] * decay


def _pad2(x, rows, cols):
    return jnp.pad(x, ((0, rows - x.shape[0]), (0, cols - x.shape[1])))


def _hyena_filters(seq_len, lp, f_w1, f_b1, f_freq1, f_w2, f_b2, f_freq2, f_w3, *, tm):
    d = D_MODEL
    t = jnp.linspace(0.0, 1.0, seq_len, dtype=F32)[:, None]
    w = (2.0 * math.pi / seq_len) * jnp.arange(seq_len, dtype=F32)[:, None]
    f = jnp.linspace(1e-4, FILTER_BANDS - 1, FILTER_BANDS, dtype=F32)[None, :]
    z = jnp.concatenate([t, jnp.cos(f * w), -jnp.sin(f * w)], axis=-1)
    z = _pad2(z, lp, FEAT_PAD)
    max_decay = math.log(DECAY_TARGET) / FAST_DECAY_PCT
    min_decay = math.log(DECAY_TARGET) / SLOW_DECAY_PCT
    deltas = jnp.abs(jnp.linspace(min_decay, max_decay, d, dtype=F32))[None, :]
    hpad = FEAT_PAD
    args = (
        z,
        _pad2(f_w1, FEAT_PAD, hpad), _pad2(f_b1[None, :], 1, hpad), _pad2(f_freq1[None, :], 1, hpad),
        _pad2(f_w2, hpad, hpad), _pad2(f_b2[None, :], 1, hpad), _pad2(f_freq2[None, :], 1, hpad),
        _pad2(f_w3, hpad, f_w3.shape[1]), deltas,
    )
    n_out = f_w3.shape[1]
    const = lambda i: (0, 0)
    return pl.pallas_call(
        _filter_body,
        grid=(lp // tm,),
        in_specs=[pl.BlockSpec((tm, FEAT_PAD), lambda i: (i, 0))]
        + [pl.BlockSpec(a.shape, const) for a in args[1:]],
        out_specs=pl.BlockSpec((tm, n_out), lambda i: (i, 0)),
        out_shape=jax.ShapeDtypeStruct((lp, n_out), F32),
        compiler_params=_cparams("parallel"),
        name="hyena_filter",
    )(*args)


def _angle(num, den):
    return (2.0 * math.pi / den) * (num % den).astype(F32)


def _dft_tables(pages):
    n1 = 2 * pages - 1
    n = n1 * PAGE
    k1 = jnp.arange(pages, dtype=jnp.int32)
    def stage1(n_in):
        a = jnp.arange(n_in, dtype=jnp.int32)
        ang = _angle(k1[:, None] * a[None, :], n1)
        return jnp.concatenate([jnp.cos(ang), -jnp.sin(ang)], axis=0).astype(BF16)
    g_data = stage1(pages)
    g_filt = stage1(n1)
    c = jnp.arange(PAGE, dtype=jnp.int32)
    k = k1[:, None, None] + n1 * c[None, :, None]
    ang = _angle(k * c[None, None, :], n)
    cs, sn = jnp.cos(ang), jnp.sin(ang)
    t_fwd = jnp.concatenate([jnp.concatenate([cs, sn], axis=2),
                             jnp.concatenate([-sn, cs], axis=2)], axis=1).astype(BF16)
    cst, snt = cs.transpose(0, 2, 1), sn.transpose(0, 2, 1)
    t_inv = jnp.concatenate([jnp.concatenate([cst, -snt], axis=2),
                             jnp.concatenate([snt, cst], axis=2)], axis=1).astype(BF16)
    a = jnp.arange(pages, dtype=jnp.int32)
    ang = _angle(a[:, None] * k1[None, :], n1)
    coef = jnp.where(k1 == 0, 1.0, 2.0)[None, :] / n
    g_inv = jnp.concatenate([coef * jnp.cos(ang), -coef * jnp.sin(ang)], axis=1).astype(BF16)
    return dict(g_data=g_data, g_filt=g_filt, t_fwd=t_fwd, t_inv=t_inv, g_inv=g_inv, n=n, n1=n1)


def _dft1_body(g_ref, z_ref, a_ref):
    a_ref[0] = jnp.dot(g_ref[...], z_ref[0], preferred_element_type=F32).astype(a_ref.dtype)


def _dft1(g, z, *, tc):
    b, pin, c = z.shape
    m = g.shape[0]
    return pl.pallas_call(
        _dft1_body,
        grid=(b, c // tc),
        in_specs=[
            pl.BlockSpec((m, pin), lambda i, j: (0, 0)),
            pl.BlockSpec((1, pin, tc), lambda i, j: (i, 0, j)),
        ],
        out_specs=pl.BlockSpec((1, m, tc), lambda i, j: (i, 0, j)),
        out_shape=jax.ShapeDtypeStruct((b, m, c), BF16),
        compiler_params=_cparams("parallel", "parallel"),
        name="dft_stage1",
    )(g, z)


def _filter_spectrum_body(t_ref, a_ref, o_ref):
    a = jnp.concatenate([a_ref[0, 0, 0], a_ref[0, 1, 0]], axis=0)
    x = jnp.dot(t_ref[0], a, preferred_element_type=F32)
    o_ref[0, 0, 0] = x[:PAGE]
    o_ref[0, 1, 0] = x[PAGE:]


def _filter_spectrum(t_fwd, af):
    o, _, pages, _, d = af.shape
    return pl.pallas_call(
        _filter_spectrum_body,
        grid=(pages, o),
        in_specs=[
            pl.BlockSpec((1, 2 * PAGE, 2 * PAGE), lambda k, i: (k, 0, 0)),
            pl.BlockSpec((1, 2, 1, PAGE, d), lambda k, i: (i, 0, k, 0, 0)),
        ],
        out_specs=pl.BlockSpec((1, 2, 1, PAGE, d), lambda k, i: (i, 0, k, 0, 0)),
        out_shape=jax.ShapeDtypeStruct(af.shape, F32),
        compiler_params=_cparams("parallel", "parallel"),
        name="filter_spectrum",
    )(t_fwd, af)


def _spectral_body(t_ref, ti_ref, kf_ref, a_ref, o_ref, *, bb):
    kr = kf_ref[0, 0]
    ki = kf_ref[1, 0]
    for b in range(bb):
        a = jnp.concatenate([a_ref[b, 0, 0], a_ref[b, 1, 0]], axis=0)
        x = jnp.dot(t_ref[0], a, preferred_element_type=F32)
        xr, xi = x[:PAGE], x[PAGE:]
        y = jnp.concatenate([xr * kr - xi * ki, xr * ki + xi * kr], axis=0).astype(BF16)
        r = jnp.dot(ti_ref[0], y, preferred_element_type=F32)
        o_ref[b, 0, 0] = r[:PAGE].astype(o_ref.dtype)
        o_ref[b, 1, 0] = r[PAGE:].astype(o_ref.dtype)


def _spectral(t_fwd, t_inv, kf, a5, *, bb):
    b, _, pages, _, d = a5.shape
    return pl.pallas_call(
        functools.partial(_spectral_body, bb=bb),
        grid=(pages, b // bb),
        in_specs=[
            pl.BlockSpec((1, 2 * PAGE, 2 * PAGE), lambda k, i: (k, 0, 0)),
            pl.BlockSpec((1, 2 * PAGE, 2 * PAGE), lambda k, i: (k, 0, 0)),
            pl.BlockSpec((2, 1, PAGE, d), lambda k, i: (0, k, 0, 0)),
            pl.BlockSpec((bb, 2, 1, PAGE, d), lambda k, i: (i, 0, k, 0, 0)),
        ],
        out_specs=pl.BlockSpec((bb, 2, 1, PAGE, d), lambda k, i: (i, 0, k, 0, 0)),
        out_shape=jax.ShapeDtypeStruct(a5.shape, BF16),
        compiler_params=_cparams("parallel", "parallel"),
        name="spectral_stage2",
    )(t_fwd, t_inv, kf, a5)


def _idft1_body(gi_ref, b_ref, z_ref, gate_ref, skip_ref, o_ref, *, seq_len, tc, pages):
    y = jnp.dot(gi_ref[...], b_ref[0], preferred_element_type=F32)
    out = gate_ref[0].astype(F32) * (y + skip_ref[...] * z_ref[0].astype(F32))
    col = pl.program_id(1) * tc + lax.broadcasted_iota(jnp.int32, (1, tc), 1)
    pos = lax.broadcasted_iota(jnp.int32, (pages, 1), 0) * PAGE + lax.shift_right_logical(col, D_MODEL.bit_length() - 1)
    o_ref[0] = jnp.where(pos < seq_len, out, 0.0).astype(o_ref.dtype)


def _idft1(g_inv, bint, z, gate, skip, *, tc, seq_len):
    b, pages, c = z.shape
    m2 = bint.shape[1]
    skip_t = jnp.tile(skip.astype(F32)[None, :], (1, tc // D_MODEL))
    return pl.pallas_call(
        functools.partial(_idft1_body, seq_len=seq_len, tc=tc, pages=pages),
        grid=(b, c // tc),
        in_specs=[
            pl.BlockSpec((pages, m2), lambda i, j: (0, 0)),
            pl.BlockSpec((1, m2, tc), lambda i, j: (i, 0, j)),
            pl.BlockSpec((1, pages, tc), lambda i, j: (i, 0, j)),
            pl.BlockSpec((1, pages, tc), lambda i, j: (i, 0, j)),
            pl.BlockSpec((1, tc), lambda i, j: (0, 0)),
        ],
        out_specs=pl.BlockSpec((1, pages, tc), lambda i, j: (i, 0, j)),
        out_shape=jax.ShapeDtypeStruct(z.shape, BF16),
        compiler_params=_cparams("parallel", "parallel"),
        name="idft_stage1",
    )(g_inv, bint, z, gate, skip_t)


def _hyena_mixer(hn_proj, conv_w, conv_b, filt, skip, tabs, *, seq_len, tm, tc, bb):
    b, lp, _ = hn_proj.shape
    d = D_MODEL
    pages = lp // PAGE
    n = tabs["n"]
    x1, x2, v = _short_conv(hn_proj, conv_w, conv_b, tm=tm, seq_len=seq_len)
    fl = filt[:seq_len].reshape(seq_len, HYENA_ORDER, 2, d)
    zeros = jnp.zeros((n - 2 * seq_len + 1, HYENA_ORDER, d), F32)
    kern = jnp.concatenate([fl[:, :, 0], zeros, fl[:0:-1, :, 1]], axis=0)
    kern = kern.transpose(1, 0, 2).astype(BF16).reshape(HYENA_ORDER, n // PAGE, PAGE * d)
    af = _dft1(tabs["g_filt"], kern, tc=tc)
    kf = _filter_spectrum(tabs["t_fwd"], af.reshape(HYENA_ORDER, 2, pages, PAGE, d))
    z = v.reshape(b, pages, PAGE * d)
    for o, gate in enumerate((x1, x2)):
        a = _dft1(tabs["g_data"], z, tc=tc)
        r = _spectral(tabs["t_fwd"], tabs["t_inv"], kf[o], a.reshape(b, 2, pages, PAGE, d), bb=bb)
        z = _idft1(tabs["g_inv"], r.reshape(b, 2 * pages, PAGE * d), z, gate.reshape(b, pages, PAGE * d),
                   skip[o], tc=tc, seq_len=seq_len)
    return z.reshape(b, lp, d)


def _t5_bucket(rel):
    half = REL_BUCKETS // 2
    max_exact = half // 2
    n = jnp.abs(rel)
    large = max_exact + (jnp.log(jnp.maximum(n, 1).astype(F32) / max_exact)
                         / math.log(REL_MAX_DIST / max_exact) * (half - max_exact)).astype(jnp.int32)
    large = jnp.minimum(large, half - 1)
    return jnp.where(rel > 0, half, 0) + jnp.where(n < max_exact, n, large)


def _bias_table_body(rb_ref, bucket_ref, o_ref):
    bk = bucket_ref[...]
    for h in range(N_HEADS):
        acc = jnp.where(bk < 0, NEG, 0.0).astype(F32)
        for r in range(REL_BUCKETS):
            acc = jnp.where(bk == r, rb_ref[r, h], acc)
        o_ref[h] = acc


def _bias_table(rel_bias, bucket, *, tr):
    r, c = bucket.shape
    return pl.pallas_call(
        _bias_table_body,
        grid=(r // tr,),
        in_specs=[
            pl.BlockSpec(memory_space=pltpu.SMEM),
            pl.BlockSpec((tr, c), lambda i: (i, 0)),
        ],
        out_specs=pl.BlockSpec((N_HEADS, tr, c), lambda i: (0, i, 0)),
        out_shape=jax.ShapeDtypeStruct((N_HEADS, r, c), F32),
        compiler_params=_cparams("parallel"),
        name="bias_table",
    )(rel_bias.astype(F32), bucket)


def _attn_body(sink_ref, q_ref, kp_ref, kc_ref, kn_ref, vp_ref, vc_ref, vn_ref, km_ref, vm_ref,
               qg_ref, kg_ref, bb_ref, mb_ref, o_ref, *, seq_len, nb):
    i = pl.program_id(1)
    hd = HEAD_DIM

    def headnorm(x, g):
        ms = jnp.mean(x * x, axis=-1, keepdims=True)
        return x * lax.rsqrt(ms + EPS) * g

    kpos = (i - 1) * BLOCK + lax.broadcasted_iota(jnp.int32, (1, 3 * BLOCK), 1)
    kvalid = (kpos >= N_META) & (kpos < seq_len)
    qg = qg_ref[...] * (hd ** -0.5)
    kg = kg_ref[...]
    for g in range(N_KV_HEADS):
        ks = slice(g * hd, (g + 1) * hd)
        kb = jnp.concatenate([kp_ref[0, :, ks], kc_ref[0, :, ks], kn_ref[0, :, ks]], axis=0).astype(F32)
        kb = headnorm(kb, kg).astype(BF16)
        km = headnorm(km_ref[0, :, ks].astype(F32), kg).astype(BF16)
        vb = jnp.concatenate([vp_ref[0, :, ks], vc_ref[0, :, ks], vn_ref[0, :, ks]], axis=0)
        vm = vm_ref[0, :, ks]
        for jj in range(GROUP):
            h = g * GROUP + jj
            q = headnorm(q_ref[0, :, h * hd:(h + 1) * hd].astype(F32), qg).astype(BF16)
            sb = lax.dot_general(q, kb, (((1,), (1,)), ((), ())), preferred_element_type=F32) + bb_ref[h]
            sb = jnp.where(kvalid, sb, NEG)
            sm = lax.dot_general(q, km, (((1,), (1,)), ((), ())), preferred_element_type=F32) + mb_ref[h]
            sink = sink_ref[h]
            m = jnp.maximum(jnp.maximum(jnp.max(sb, axis=-1, keepdims=True),
                                        jnp.max(sm, axis=-1, keepdims=True)), sink)
            pb = jnp.exp(sb - m)
            pm = jnp.exp(sm - m)
            den = (jnp.sum(pb, axis=-1, keepdims=True) + jnp.sum(pm, axis=-1, keepdims=True)
                   + jnp.exp(sink - m))
            o = (jnp.dot(pb.astype(BF16), vb, preferred_element_type=F32)
                 + jnp.dot(pm.astype(BF16), vm, preferred_element_type=F32)) / den
            o_ref[0, :, h * hd:(h + 1) * hd] = o.astype(o_ref.dtype)


def _attention(qkv, q_gain, k_gain, sink, band_bias, meta_bias, *, seq_len):
    b, lp, _ = qkv.shape
    nb = lp // BLOCK
    dq = N_HEADS * HEAD_DIM
    dkv = N_KV_HEADS * HEAD_DIM
    kcol = dq // dkv
    vcol = kcol + 1
    prv = lambda i, j: jnp.maximum(j - 1, 0)
    nxt = lambda i, j: jnp.minimum(j + 1, nb - 1)
    blk = lambda rows, col: pl.BlockSpec((1, BLOCK, dkv), lambda i, j: (i, rows(i, j), col))
    cur = lambda i, j: j
    return pl.pallas_call(
        functools.partial(_attn_body, seq_len=seq_len, nb=nb),
        grid=(b, nb),
        in_specs=[
            pl.BlockSpec(memory_space=pltpu.SMEM),
            pl.BlockSpec((1, BLOCK, dq), lambda i, j: (i, j, 0)),
            blk(prv, kcol), blk(cur, kcol), blk(nxt, kcol),
            blk(prv, vcol), blk(cur, vcol), blk(nxt, vcol),
            pl.BlockSpec((1, N_META, dkv), lambda i, j: (i, 0, kcol)),
            pl.BlockSpec((1, N_META, dkv), lambda i, j: (i, 0, vcol)),
            pl.BlockSpec((1, HEAD_DIM), lambda i, j: (0, 0)),
            pl.BlockSpec((1, HEAD_DIM), lambda i, j: (0, 0)),
            pl.BlockSpec((N_HEADS, BLOCK, 3 * BLOCK), lambda i, j: (0, 0, 0)),
            pl.BlockSpec((N_HEADS, BLOCK, N_META), lambda i, j: (0, j, 0)),
        ],
        out_specs=pl.BlockSpec((1, BLOCK, dq), lambda i, j: (i, j, 0)),
        out_shape=jax.ShapeDtypeStruct((b, lp, dq), BF16),
        compiler_params=_cparams("parallel", "parallel"),
        name="window_attention",
    )(sink.astype(F32), qkv, qkv, qkv, qkv, qkv, qkv, qkv, qkv, qkv,
      q_gain.reshape(1, HEAD_DIM).astype(F32), k_gain.reshape(1, HEAD_DIM).astype(F32), band_bias, meta_bias)


def _attention_bias_tables(rel_bias, lp):
    r = jnp.arange(BLOCK, dtype=jnp.int32)[:, None]
    s = jnp.arange(3 * BLOCK, dtype=jnp.int32)[None, :]
    rel = s - BLOCK - r
    band_bucket = jnp.where(jnp.abs(rel) <= WINDOW, _t5_bucket(rel), -1)
    qpos = jnp.arange(lp, dtype=jnp.int32)[:, None]
    meta_bucket = _t5_bucket(jnp.arange(N_META, dtype=jnp.int32)[None, :] - qpos)
    return (_bias_table(rel_bias, band_bucket, tr=BLOCK), _bias_table(rel_bias, meta_bucket, tr=_seq_tile(lp)))


def _seq_tile(lp):
    pages = lp // PAGE
    for k in (5, 4, 3, 2, 1):
        if pages % k == 0:
            return k * PAGE
    return PAGE


def _trunk(x, meta_tokens, rel_bias, mix_norm, mlp_norm, hy, at, mlp_w_up, mlp_w_down):
    b, s, d = x.shape
    seq_len = N_META + s
    pages = -(-seq_len // PAGE)
    lp = pages * PAGE
    assert seq_len <= lp - 64
    tm_seq = _seq_tile(lp)
    tm_flat = 512
    assert (b * lp) % tm_flat == 0
    tc = 8 * d
    bb = 4

    meta = jnp.broadcast_to(meta_tokens[None].astype(x.dtype), (b, N_META, d))
    h = jnp.concatenate([meta, x, jnp.zeros((b, lp - seq_len, d), x.dtype)], axis=1)

    tabs = _dft_tables(pages)
    band_bias, meta_bias = _attention_bias_tables(rel_bias, lp)

    depth = mix_norm.shape[0]
    for i in range(depth):
        j = i // 2
        if i % 2 == 0:
            u = _norm_matmul(h, mix_norm[i], hy["w_in"][j], tm=tm_seq, seq_len=seq_len)
            filt = _hyena_filters(seq_len, lp, hy["f_w1"][j], hy["f_b1"][j], hy["f_freq1"][j], hy["f_w2"][j],
                                  hy["f_b2"][j], hy["f_freq2"][j], hy["f_w3"][j], tm=tm_seq)
            z = _hyena_mixer(u, hy["conv_w"][j], hy["conv_b"][j], filt, hy["skip"][j], tabs,
                             seq_len=seq_len, tm=tm_seq, tc=tc, bb=bb)
            w_out = hy["w_out"][j]
        else:
            qkv = _norm_matmul(h, mix_norm[i], at["w_qkv"][j], tm=tm_seq)
            z = _attention(qkv, at["q_norm"][j], at["k_norm"][j], at["sink"][j], band_bias, meta_bias,
                           seq_len=seq_len)
            w_out = at["w_out"][j]
        hf = _proj_residual(h.reshape(b * lp, d), z.reshape(b * lp, d), w_out, tm=tm_flat)
        hf = _mlp(hf, mlp_norm[i], mlp_w_up[i], mlp_w_down[i], tm=tm_flat)
        h = hf.reshape(b, lp, d)
    return h[:, N_META:seq_len]


def kernel(x_prompt, x_sample, meta_tokens, rel_bias, mix_norm, mlp_norm, hy_w_in, hy_conv_w, hy_conv_b, hy_f_w1, hy_f_b1, hy_f_freq1, hy_f_w2, hy_f_b2, hy_f_freq2, hy_f_w3, hy_skip, hy_w_out, at_w_qkv, at_q_norm, at_k_norm, at_sink, at_w_out, mlp_w_up, mlp_w_down):
    hy = dict(w_in=hy_w_in.astype(BF16), conv_w=hy_conv_w, conv_b=hy_conv_b, f_w1=hy_f_w1, f_b1=hy_f_b1,
              f_freq1=hy_f_freq1, f_w2=hy_f_w2, f_b2=hy_f_b2, f_freq2=hy_f_freq2, f_w3=hy_f_w3,
              skip=hy_skip, w_out=hy_w_out.astype(BF16))
    at = dict(w_qkv=at_w_qkv.astype(BF16), q_norm=at_q_norm, k_norm=at_k_norm, sink=at_sink,
              w_out=at_w_out.astype(BF16))
    w_up = mlp_w_up.astype(BF16)
    w_down = mlp_w_down.astype(BF16)
    run = functools.partial(_trunk, meta_tokens=meta_tokens, rel_bias=rel_bias, mix_norm=mix_norm,
                            mlp_norm=mlp_norm, hy=hy, at=at, mlp_w_up=w_up, mlp_w_down=w_down)
    return (run(x_prompt), run(x_sample))
```

```python
import functools
import math

import jax
import jax.numpy as jnp
from jax import lax
from jax.experimental import pallas as pl
from jax.experimental.pallas import tpu as pltpu

F32 = jnp.float32
BF16 = jnp.bfloat16

D_MODEL = 1024
N_META = 16
HYENA_ORDER = 2
FILTER_EMB = 33
FILTER_HIDDEN = 64
FILTER_BANDS = (FILTER_EMB - 1) // 2
DECAY_TARGET = 1e-2
FAST_DECAY_PCT = 0.3
SLOW_DECAY_PCT = 1.5
N_HEADS = 16
N_KV_HEADS = 4
HEAD_DIM = D_MODEL // N_HEADS
GROUP = N_HEADS // N_KV_HEADS
WINDOW = 128
BLOCK = 128
REL_BUCKETS = 32
REL_MAX_DIST = 128
D_FF = 4 * D_MODEL
EPS = 1e-6
NEG = -1e30

PAGE = 128
LANES = 128
FEAT_PAD = LANES
VMEM_LIMIT_BYTES = 52 * 1024 * 1024
HALO = 16
DFT_ROWS = 32
DFT_COLS = 512


def _cparams(*sem):
    return pltpu.CompilerParams(dimension_semantics=sem, vmem_limit_bytes=VMEM_LIMIT_BYTES)


def _rms(x, g):
    ms = jnp.mean(x * x, axis=-1, keepdims=True)
    return (x * lax.rsqrt(ms + EPS)) * g


def _norm_matmul_body(x_ref, g_ref, w_ref, o_ref, *, seq_len, tm):
    xn = _rms(x_ref[0], g_ref[...])
    y = jnp.dot(xn.astype(BF16), w_ref[...], preferred_element_type=F32)
    if seq_len is not None:
        row = pl.program_id(1) * tm + lax.broadcasted_iota(jnp.int32, (tm, 1), 0)
        y = jnp.where(row < seq_len, y, 0.0)
    o_ref[0] = y.astype(o_ref.dtype)


def _norm_matmul(h, g, w, *, tm, seq_len=None):
    b, lp, d = h.shape
    n = w.shape[1]
    return pl.pallas_call(
        functools.partial(_norm_matmul_body, seq_len=seq_len, tm=tm),
        grid=(b, lp // tm),
        in_specs=[
            pl.BlockSpec((1, tm, d), lambda i, j: (i, j, 0)),
            pl.BlockSpec((1, d), lambda i, j: (0, 0)),
            pl.BlockSpec((d, n), lambda i, j: (0, 0)),
        ],
        out_specs=pl.BlockSpec((1, tm, n), lambda i, j: (i, j, 0)),
        out_shape=jax.ShapeDtypeStruct((b, lp, n), BF16),
        compiler_params=_cparams("parallel", "parallel"),
        name="norm_matmul",
    )(h, g.reshape(1, d), w)


def _proj_residual_body(h_ref, z_ref, w_ref, o_ref):
    o_ref[...] = h_ref[...] + jnp.dot(z_ref[...], w_ref[...], preferred_element_type=F32)


def _proj_residual(h, z, w, *, tm):
    m, d = h.shape
    k = z.shape[1]
    return pl.pallas_call(
        _proj_residual_body,
        grid=(m // tm,),
        in_specs=[
            pl.BlockSpec((tm, d), lambda i: (i, 0)),
            pl.BlockSpec((tm, k), lambda i: (i, 0)),
            pl.BlockSpec((k, d), lambda i: (0, 0)),
        ],
        out_specs=pl.BlockSpec((tm, d), lambda i: (i, 0)),
        out_shape=jax.ShapeDtypeStruct((m, d), F32),
        compiler_params=_cparams("parallel"),
        name="proj_residual",
    )(h, z, w)


def _mlp_body(x_ref, g_ref, wu_ref, wd_ref, o_ref, *, f_chunk):
    x = x_ref[...]
    xb = _rms(x, g_ref[...]).astype(BF16)
    acc = x
    for f in range(D_FF // f_chunk):
        u = jnp.dot(xb, wu_ref[:, f * f_chunk:(f + 1) * f_chunk], preferred_element_type=F32)
        a = jnp.square(jnp.maximum(u, 0.0)).astype(BF16)
        acc = acc + jnp.dot(a, wd_ref[f * f_chunk:(f + 1) * f_chunk, :], preferred_element_type=F32)
    o_ref[...] = acc


def _mlp(h, g, w_up, w_down, *, tm, f_chunk=1024):
    m, d = h.shape
    return pl.pallas_call(
        functools.partial(_mlp_body, f_chunk=f_chunk),
        grid=(m // tm,),
        in_specs=[
            pl.BlockSpec((tm, d), lambda i: (i, 0)),
            pl.BlockSpec((1, d), lambda i: (0, 0)),
            pl.BlockSpec((d, D_FF), lambda i: (0, 0)),
            pl.BlockSpec((D_FF, d), lambda i: (0, 0)),
        ],
        out_specs=pl.BlockSpec((tm, d), lambda i: (i, 0)),
        out_shape=jax.ShapeDtypeStruct((m, d), F32),
        compiler_params=_cparams("parallel"),
        name="mlp",
    )(h, g.reshape(1, d), w_up, w_down)


def _short_conv_body(prev_ref, cur_ref, next_ref, w_ref, b_ref, o_ref, *, seq_len, tm):
    j = pl.program_id(1)
    x = cur_ref[0].astype(F32)
    prev_row = jnp.where(j > 0, prev_ref[0].astype(F32)[HALO - 1:HALO, :], 0.0)
    next_row = next_ref[0].astype(F32)[0:1, :]
    rid = lax.broadcasted_iota(jnp.int32, (tm, 1), 0)
    up = jnp.where(rid == 0, prev_row, pltpu.roll(x, 1, 0))
    dn = jnp.where(rid == tm - 1, next_row, pltpu.roll(x, tm - 1, 0))
    y = up * w_ref[0, 0:1, :] + x * w_ref[0, 1:2, :] + dn * w_ref[0, 2:3, :] + b_ref[0]
    y = jnp.where(j * tm + rid < seq_len, y, 0.0)
    o_ref[0, 0] = y.astype(o_ref.dtype)


def _short_conv(u, w, bias, *, tm, seq_len):
    b, lp, d3 = u.shape
    d = d3 // 3
    nh = tm // HALO
    last = lp // HALO - 1
    w3 = w.reshape(3, 3, d).transpose(1, 0, 2)
    b3 = bias.reshape(3, 1, d)
    return pl.pallas_call(
        functools.partial(_short_conv_body, seq_len=seq_len, tm=tm),
        grid=(b, lp // tm, 3),
        in_specs=[
            pl.BlockSpec((1, HALO, d), lambda i, j, c: (i, jnp.maximum(j * nh - 1, 0), c)),
            pl.BlockSpec((1, tm, d), lambda i, j, c: (i, j, c)),
            pl.BlockSpec((1, HALO, d), lambda i, j, c: (i, jnp.minimum((j + 1) * nh, last), c)),
            pl.BlockSpec((1, 3, d), lambda i, j, c: (c, 0, 0)),
            pl.BlockSpec((1, 1, d), lambda i, j, c: (c, 0, 0)),
        ],
        out_specs=pl.BlockSpec((1, 1, tm, d), lambda i, j, c: (c, i, j, 0)),
        out_shape=jax.ShapeDtypeStruct((3, b, lp, d), BF16),
        compiler_params=_cparams("parallel", "parallel", "parallel"),
        name="short_conv",
    )(u, u, u, w3, b3)


def _filter_body(z_ref, w1_ref, b1_ref, fr1_ref, w2_ref, b2_ref, fr2_ref, w3_ref, delta_ref, o_ref,
                 *, seq_len, n, tm):
    hp = lax.Precision.HIGHEST
    z = z_ref[...]
    h = jnp.sin(fr1_ref[...] * (jnp.dot(z, w1_ref[...], precision=hp, preferred_element_type=F32) + b1_ref[...]))
    h = jnp.sin(fr2_ref[...] * (jnp.dot(h, w2_ref[...], precision=hp, preferred_element_type=F32) + b2_ref[...]))
    k = jnp.dot(h, w3_ref[...], precision=hp, preferred_element_type=F32)
    decay = jnp.exp(-z[:, 0:1] * delta_ref[...])
    d = delta_ref.shape[1]
    p = pl.program_id(0) * tm + lax.broadcasted_iota(jnp.int32, (tm, 1), 0)
    fwd = p < seq_len
    bwd = p > n - seq_len
    for o in range(HYENA_ORDER):
        kf = k[:, (2 * o) * d:(2 * o + 1) * d]
        kb = k[:, (2 * o + 1) * d:(2 * o + 2) * d]
        o_ref[o] = (jnp.where(fwd, kf, jnp.where(bwd, kb, 0.0)) * decay).astype(o_ref.dtype)


def _pad2(x, rows, cols):
    return jnp.pad(x, ((0, rows - x.shape[0]), (0, cols - x.shape[1])))


def _hyena_kernels(seq_len, n, f_w1, f_b1, f_freq1, f_w2, f_b2, f_freq2, f_w3):
    d = D_MODEL
    tm = _seq_tile(n)
    p = jnp.arange(n, dtype=jnp.int32)
    lag = jnp.where(p < seq_len, p, jnp.where(p > n - seq_len, n - p, 0))
    t = jnp.linspace(0.0, 1.0, seq_len, dtype=F32)[lag][:, None]
    w = ((2.0 * math.pi / seq_len) * jnp.arange(seq_len, dtype=F32))[lag][:, None]
    f = jnp.linspace(1e-4, FILTER_BANDS - 1, FILTER_BANDS, dtype=F32)[None, :]
    z = _pad2(jnp.concatenate([t, jnp.cos(f * w), -jnp.sin(f * w)], axis=-1), n, FEAT_PAD)
    max_decay = math.log(DECAY_TARGET) / FAST_DECAY_PCT
    min_decay = math.log(DECAY_TARGET) / SLOW_DECAY_PCT
    deltas = jnp.abs(jnp.linspace(min_decay, max_decay, d, dtype=F32))[None, :]
    hpad = FEAT_PAD
    args = (
        z,
        _pad2(f_w1, FEAT_PAD, hpad), _pad2(f_b1[None, :], 1, hpad), _pad2(f_freq1[None, :], 1, hpad),
        _pad2(f_w2, hpad, hpad), _pad2(f_b2[None, :], 1, hpad), _pad2(f_freq2[None, :], 1, hpad),
        _pad2(f_w3, hpad, f_w3.shape[1]), deltas,
    )
    const = lambda i: (0, 0)
    return pl.pallas_call(
        functools.partial(_filter_body, seq_len=seq_len, n=n, tm=tm),
        grid=(n // tm,),
        in_specs=[pl.BlockSpec((tm, FEAT_PAD), lambda i: (i, 0))]
        + [pl.BlockSpec(a.shape, const) for a in args[1:]],
        out_specs=pl.BlockSpec((HYENA_ORDER, tm, d), lambda i: (0, i, 0)),
        out_shape=jax.ShapeDtypeStruct((HYENA_ORDER, n, d), BF16),
        compiler_params=_cparams("parallel"),
        name="hyena_filter",
    )(*args)


def _angle(num, den):
    return (2.0 * math.pi / den) * (num % den).astype(F32)


def _dft_tables(pages):
    n1 = 2 * pages - 1
    n = n1 * PAGE
    k1 = jnp.arange(pages, dtype=jnp.int32)
    def stage1(n_in):
        a = jnp.arange(n_in, dtype=jnp.int32)
        ang = _angle(k1[:, None] * a[None, :], n1)
        return jnp.concatenate([jnp.cos(ang), -jnp.sin(ang)], axis=0).astype(BF16)
    g_data = stage1(pages)
    g_filt = stage1(n1)
    c = jnp.arange(PAGE, dtype=jnp.int32)
    k = k1[:, None, None] + n1 * c[None, :, None]
    ang = _angle(k * c[None, None, :], n)
    cs, sn = jnp.cos(ang), jnp.sin(ang)
    t_fwd = jnp.concatenate([jnp.concatenate([cs, sn], axis=2),
                             jnp.concatenate([-sn, cs], axis=2)], axis=1).astype(BF16)
    cst, snt = cs.transpose(0, 2, 1), sn.transpose(0, 2, 1)
    t_inv = jnp.concatenate([jnp.concatenate([cst, -snt], axis=2),
                             jnp.concatenate([snt, cst], axis=2)], axis=1).astype(BF16)
    a = jnp.arange(pages, dtype=jnp.int32)
    ang = _angle(a[:, None] * k1[None, :], n1)
    coef = jnp.where(k1 == 0, 1.0, 2.0)[None, :] / n
    g_inv = jnp.concatenate([coef * jnp.cos(ang), -coef * jnp.sin(ang)], axis=1).astype(BF16)
    return dict(g_data=g_data, g_filt=g_filt, t_fwd=t_fwd, t_inv=t_inv, g_inv=g_inv, n=n, n1=n1)


def _page_gather(bufs, c, rows, cc):
    return jnp.concatenate([buf[pl.ds(c, rows, stride=cc), :] for buf in bufs], axis=1)


def _dft1_body(g_ref, z_ref, a_ref, *scratch, pin, pout, cc, dd):
    nl = dd // LANES
    xs, rs = scratch[:nl], scratch[nl:]
    for l in range(nl):
        xs[l][...] = z_ref[0, :, :, l * LANES:(l + 1) * LANES].astype(F32).reshape(pin * cc, LANES)

    def step(c, carry):
        x = _page_gather(xs, c, pin, cc).astype(BF16)
        r = jnp.dot(g_ref[...], x, preferred_element_type=F32)
        for l in range(nl):
            rs[l][pl.ds(c, 2 * pout, stride=cc), :] = r[:, l * LANES:(l + 1) * LANES]
        return carry

    lax.fori_loop(0, cc, step, 0, unroll=4)
    for l in range(nl):
        a_ref[0, :, :, :, l * LANES:(l + 1) * LANES] = rs[l][...].reshape(2, pout, cc, LANES).astype(a_ref.dtype)


def _dft1(g, z, *, cc, dd):
    b, pin, _, d = z.shape
    pout = g.shape[0] // 2
    nl = dd // LANES
    return pl.pallas_call(
        functools.partial(_dft1_body, pin=pin, pout=pout, cc=cc, dd=dd),
        grid=(b, PAGE // cc, d // dd),
        in_specs=[
            pl.BlockSpec(g.shape, lambda i, j, k: (0, 0)),
            pl.BlockSpec((1, pin, cc, dd), lambda i, j, k: (i, 0, j, k)),
        ],
        out_specs=pl.BlockSpec((1, 2, pout, cc, dd), lambda i, j, k: (i, 0, 0, j, k)),
        out_shape=jax.ShapeDtypeStruct((b, 2, pout, PAGE, d), BF16),
        scratch_shapes=[pltpu.VMEM((pin * cc, LANES), F32)] * nl + [pltpu.VMEM((2 * pout * cc, LANES), F32)] * nl,
        compiler_params=_cparams("parallel", "parallel", "parallel"),
        name="dft_stage1",
    )(g, z)


def _filter_spectrum_body(t_ref, a_ref, o_ref):
    a = jnp.concatenate([a_ref[0, 0, 0], a_ref[0, 1, 0]], axis=0)
    x = jnp.dot(t_ref[0], a, preferred_element_type=F32)
    o_ref[0, 0, 0] = x[:PAGE]
    o_ref[0, 1, 0] = x[PAGE:]


def _filter_spectrum(t_fwd, af):
    o, _, pages, _, d = af.shape
    return pl.pallas_call(
        _filter_spectrum_body,
        grid=(pages, o),
        in_specs=[
            pl.BlockSpec((1, 2 * PAGE, 2 * PAGE), lambda k, i: (k, 0, 0)),
            pl.BlockSpec((1, 2, 1, PAGE, d), lambda k, i: (i, 0, k, 0, 0)),
        ],
        out_specs=pl.BlockSpec((1, 2, 1, PAGE, d), lambda k, i: (i, 0, k, 0, 0)),
        out_shape=jax.ShapeDtypeStruct(af.shape, F32),
        compiler_params=_cparams("parallel", "parallel"),
        name="filter_spectrum",
    )(t_fwd, af)


def _spectral_body(t_ref, ti_ref, kf_ref, a_ref, o_ref, *, bb):
    kr = kf_ref[0, 0]
    ki = kf_ref[1, 0]
    for b in range(bb):
        a = jnp.concatenate([a_ref[b, 0, 0], a_ref[b, 1, 0]], axis=0)
        x = jnp.dot(t_ref[0], a, preferred_element_type=F32)
        xr, xi = x[:PAGE], x[PAGE:]
        y = jnp.concatenate([xr * kr - xi * ki, xr * ki + xi * kr], axis=0).astype(BF16)
        r = jnp.dot(ti_ref[0], y, preferred_element_type=F32)
        o_ref[b, 0, 0] = r[:PAGE].astype(o_ref.dtype)
        o_ref[b, 1, 0] = r[PAGE:].astype(o_ref.dtype)


def _spectral(t_fwd, t_inv, kf, a5, *, bb):
    b, _, pages, _, d = a5.shape
    return pl.pallas_call(
        functools.partial(_spectral_body, bb=bb),
        grid=(pages, b // bb),
        in_specs=[
            pl.BlockSpec((1, 2 * PAGE, 2 * PAGE), lambda k, i: (k, 0, 0)),
            pl.BlockSpec((1, 2 * PAGE, 2 * PAGE), lambda k, i: (k, 0, 0)),
            pl.BlockSpec((2, 1, PAGE, d), lambda k, i: (0, k, 0, 0)),
            pl.BlockSpec((bb, 2, 1, PAGE, d), lambda k, i: (i, 0, k, 0, 0)),
        ],
        out_specs=pl.BlockSpec((bb, 2, 1, PAGE, d), lambda k, i: (i, 0, k, 0, 0)),
        out_shape=jax.ShapeDtypeStruct(a5.shape, BF16),
        compiler_params=_cparams("parallel", "parallel"),
        name="spectral_stage2",
    )(t_fwd, t_inv, kf, a5)


def _idft1_body(gi_ref, b_ref, z_ref, gate_ref, skip_ref, o_ref, *scratch, seq_len, pages, cc, dd):
    nl = dd // LANES
    bs, ys = scratch[:nl], scratch[nl:]
    for l in range(nl):
        bs[l][...] = b_ref[0, :, :, :, l * LANES:(l + 1) * LANES].astype(F32).reshape(2 * pages * cc, LANES)

    def step(c, carry):
        x = _page_gather(bs, c, 2 * pages, cc).astype(BF16)
        y = jnp.dot(gi_ref[...], x, preferred_element_type=F32)
        for l in range(nl):
            ys[l][pl.ds(c, pages, stride=cc), :] = y[:, l * LANES:(l + 1) * LANES]
        return carry

    lax.fori_loop(0, cc, step, 0, unroll=4)
    pos = (lax.broadcasted_iota(jnp.int32, (pages, cc, 1), 0) * PAGE + pl.program_id(1) * cc
           + lax.broadcasted_iota(jnp.int32, (pages, cc, 1), 1))
    for l in range(nl):
        ls = slice(l * LANES, (l + 1) * LANES)
        y = ys[l][...].reshape(pages, cc, LANES)
        out = gate_ref[0, :, :, ls].astype(F32) * (y + skip_ref[:, ls] * z_ref[0, :, :, ls].astype(F32))
        o_ref[0, :, :, ls] = jnp.where(pos < seq_len, out, 0.0).astype(o_ref.dtype)


def _idft1(g_inv, bint, z, gate, skip, *, cc, dd, seq_len):
    b, pages, _, d = z.shape
    nl = dd // LANES
    return pl.pallas_call(
        functools.partial(_idft1_body, seq_len=seq_len, pages=pages, cc=cc, dd=dd),
        grid=(b, PAGE // cc, d // dd),
        in_specs=[
            pl.BlockSpec(g_inv.shape, lambda i, j, k: (0, 0)),
            pl.BlockSpec((1, 2, pages, cc, dd), lambda i, j, k: (i, 0, 0, j, k)),
            pl.BlockSpec((1, pages, cc, dd), lambda i, j, k: (i, 0, j, k)),
            pl.BlockSpec((1, pages, cc, dd), lambda i, j, k: (i, 0, j, k)),
            pl.BlockSpec((1, dd), lambda i, j, k: (0, k)),
        ],
        out_specs=pl.BlockSpec((1, pages, cc, dd), lambda i, j, k: (i, 0, j, k)),
        out_shape=jax.ShapeDtypeStruct(z.shape, BF16),
        scratch_shapes=[pltpu.VMEM((2 * pages * cc, LANES), F32)] * nl + [pltpu.VMEM((pages * cc, LANES), F32)] * nl,
        compiler_params=_cparams("parallel", "parallel", "parallel"),
        name="idft_stage1",
    )(g_inv, bint, z, gate, skip.astype(F32).reshape(1, d))


def _hyena_mixer(hn_proj, conv_w, conv_b, kern, skip, tabs, *, seq_len, tm, cc, dd, bb):
    b, lp, _ = hn_proj.shape
    d = D_MODEL
    pages = lp // PAGE
    x1, x2, v = (t.reshape(b, pages, PAGE, d) for t in _short_conv(hn_proj, conv_w, conv_b, tm=tm, seq_len=seq_len))
    af = _dft1(tabs["g_filt"], kern.reshape(HYENA_ORDER, tabs["n1"], PAGE, d), cc=cc, dd=dd)
    kf = _filter_spectrum(tabs["t_fwd"], af)
    z = v
    for o, gate in enumerate((x1, x2)):
        a = _dft1(tabs["g_data"], z, cc=cc, dd=dd)
        r = _spectral(tabs["t_fwd"], tabs["t_inv"], kf[o], a, bb=bb)
        z = _idft1(tabs["g_inv"], r, z, gate, skip[o], cc=cc, dd=dd, seq_len=seq_len)
    return z.reshape(b, lp, d)


def _t5_bucket(rel):
    half = REL_BUCKETS // 2
    max_exact = half // 2
    n = jnp.abs(rel)
    large = max_exact + (jnp.log(jnp.maximum(n, 1).astype(F32) / max_exact)
                         / math.log(REL_MAX_DIST / max_exact) * (half - max_exact)).astype(jnp.int32)
    large = jnp.minimum(large, half - 1)
    return jnp.where(rel > 0, half, 0) + jnp.where(n < max_exact, n, large)


def _bias_table_body(rb_ref, bucket_ref, o_ref):
    bk = bucket_ref[...]
    for h in range(N_HEADS):
        acc = jnp.where(bk < 0, NEG, 0.0).astype(F32)
        for r in range(REL_BUCKETS):
            acc = jnp.where(bk == r, rb_ref[r, h], acc)
        o_ref[h] = acc


def _bias_table(rel_bias, bucket, *, tr):
    r, c = bucket.shape
    return pl.pallas_call(
        _bias_table_body,
        grid=(r // tr,),
        in_specs=[
            pl.BlockSpec(memory_space=pltpu.SMEM),
            pl.BlockSpec((tr, c), lambda i: (i, 0)),
        ],
        out_specs=pl.BlockSpec((N_HEADS, tr, c), lambda i: (0, i, 0)),
        out_shape=jax.ShapeDtypeStruct((N_HEADS, r, c), F32),
        compiler_params=_cparams("parallel"),
        name="bias_table",
    )(rel_bias.astype(F32), bucket)


def _attn_body(sink_ref, q_ref, kp_ref, kc_ref, kn_ref, vp_ref, vc_ref, vn_ref, km_ref, vm_ref,
               qg_ref, kg_ref, bb_ref, mb_ref, o_ref, *, seq_len, nb):
    i = pl.program_id(1)
    hd = HEAD_DIM

    def headnorm(x, g):
        ms = jnp.mean(x * x, axis=-1, keepdims=True)
        return x * lax.rsqrt(ms + EPS) * g

    kpos = (i - 1) * BLOCK + lax.broadcasted_iota(jnp.int32, (1, 3 * BLOCK), 1)
    kvalid = (kpos >= N_META) & (kpos < seq_len)
    qg = qg_ref[...] * (hd ** -0.5)
    kg = kg_ref[...]
    for g in range(N_KV_HEADS):
        ks = slice(g * hd, (g + 1) * hd)
        kb = jnp.concatenate([kp_ref[0, :, ks], kc_ref[0, :, ks], kn_ref[0, :, ks]], axis=0).astype(F32)
        kb = headnorm(kb, kg).astype(BF16)
        km = headnorm(km_ref[0, :, ks].astype(F32), kg).astype(BF16)
        vb = jnp.concatenate([vp_ref[0, :, ks], vc_ref[0, :, ks], vn_ref[0, :, ks]], axis=0)
        vm = vm_ref[0, :, ks]
        for jj in range(GROUP):
            h = g * GROUP + jj
            q = headnorm(q_ref[0, :, h * hd:(h + 1) * hd].astype(F32), qg).astype(BF16)
            sb = lax.dot_general(q, kb, (((1,), (1,)), ((), ())), preferred_element_type=F32) + bb_ref[h]
            sb = jnp.where(kvalid, sb, NEG)
            sm = lax.dot_general(q, km, (((1,), (1,)), ((), ())), preferred_element_type=F32) + mb_ref[h]
            sink = sink_ref[h]
            m = jnp.maximum(jnp.maximum(jnp.max(sb, axis=-1, keepdims=True),
                                        jnp.max(sm, axis=-1, keepdims=True)), sink)
            pb = jnp.exp(sb - m)
            pm = jnp.exp(sm - m)
            den = (jnp.sum(pb, axis=-1, keepdims=True) + jnp.sum(pm, axis=-1, keepdims=True)
                   + jnp.exp(sink - m))
            o = (jnp.dot(pb.astype(BF16), vb, preferred_element_type=F32)
                 + jnp.dot(pm.astype(BF16), vm, preferred_element_type=F32)) / den
            o_ref[0, :, h * hd:(h + 1) * hd] = o.astype(o_ref.dtype)


def _attention(qkv, q_gain, k_gain, sink, band_bias, meta_bias, *, seq_len):
    b, lp, _ = qkv.shape
    nb = lp // BLOCK
    dq = N_HEADS * HEAD_DIM
    dkv = N_KV_HEADS * HEAD_DIM
    kcol = dq // dkv
    vcol = kcol + 1
    prv = lambda i, j: jnp.maximum(j - 1, 0)
    nxt = lambda i, j: jnp.minimum(j + 1, nb - 1)
    blk = lambda rows, col: pl.BlockSpec((1, BLOCK, dkv), lambda i, j: (i, rows(i, j), col))
    cur = lambda i, j: j
    return pl.pallas_call(
        functools.partial(_attn_body, seq_len=seq_len, nb=nb),
        grid=(b, nb),
        in_specs=[
            pl.BlockSpec(memory_space=pltpu.SMEM),
            pl.BlockSpec((1, BLOCK, dq), lambda i, j: (i, j, 0)),
            blk(prv, kcol), blk(cur, kcol), blk(nxt, kcol),
            blk(prv, vcol), blk(cur, vcol), blk(nxt, vcol),
            pl.BlockSpec((1, N_META, dkv), lambda i, j: (i, 0, kcol)),
            pl.BlockSpec((1, N_META, dkv), lambda i, j: (i, 0, vcol)),
            pl.BlockSpec((1, HEAD_DIM), lambda i, j: (0, 0)),
            pl.BlockSpec((1, HEAD_DIM), lambda i, j: (0, 0)),
            pl.BlockSpec((N_HEADS, BLOCK, 3 * BLOCK), lambda i, j: (0, 0, 0)),
            pl.BlockSpec((N_HEADS, BLOCK, N_META), lambda i, j: (0, j, 0)),
        ],
        out_specs=pl.BlockSpec((1, BLOCK, dq), lambda i, j: (i, j, 0)),
        out_shape=jax.ShapeDtypeStruct((b, lp, dq), BF16),
        compiler_params=_cparams("parallel", "parallel"),
        name="window_attention",
    )(sink.astype(F32), qkv, qkv, qkv, qkv, qkv, qkv, qkv, qkv, qkv,
      q_gain.reshape(1, HEAD_DIM).astype(F32), k_gain.reshape(1, HEAD_DIM).astype(F32), band_bias, meta_bias)


def _attention_bias_tables(rel_bias, lp):
    r = jnp.arange(BLOCK, dtype=jnp.int32)[:, None]
    s = jnp.arange(3 * BLOCK, dtype=jnp.int32)[None, :]
    rel = s - BLOCK - r
    band_bucket = jnp.where(jnp.abs(rel) <= WINDOW, _t5_bucket(rel), -1)
    qpos = jnp.arange(lp, dtype=jnp.int32)[:, None]
    meta_bucket = _t5_bucket(jnp.arange(N_META, dtype=jnp.int32)[None, :] - qpos)
    return (_bias_table(rel_bias, band_bucket, tr=BLOCK), _bias_table(rel_bias, meta_bucket, tr=_seq_tile(lp)))


def _seq_tile(lp):
    pages = lp // PAGE
    for k in (5, 4, 3, 2, 1):
        if pages % k == 0:
            return k * PAGE
    return PAGE


def _trunk(x, meta_tokens, rel_bias, mix_norm, mlp_norm, hy, at, mlp_w_up, mlp_w_down):
    b, s, d = x.shape
    seq_len = N_META + s
    pages = -(-seq_len // PAGE)
    lp = pages * PAGE
    assert seq_len <= lp - 64
    tm_seq = _seq_tile(lp)
    tm_flat = 512
    assert (b * lp) % tm_flat == 0
    bb = 4

    meta = jnp.broadcast_to(meta_tokens[None].astype(x.dtype), (b, N_META, d))
    h = jnp.concatenate([meta, x, jnp.zeros((b, lp - seq_len, d), x.dtype)], axis=1)

    tabs = _dft_tables(pages)
    band_bias, meta_bias = _attention_bias_tables(rel_bias, lp)

    depth = mix_norm.shape[0]
    for i in range(depth):
        j = i // 2
        if i % 2 == 0:
            u = _norm_matmul(h, mix_norm[i], hy["w_in"][j], tm=tm_seq, seq_len=seq_len)
            kern = _hyena_kernels(seq_len, tabs["n"], hy["f_w1"][j], hy["f_b1"][j], hy["f_freq1"][j], hy["f_w2"][j],
                                  hy["f_b2"][j], hy["f_freq2"][j], hy["f_w3"][j])
            z = _hyena_mixer(u, hy["conv_w"][j], hy["conv_b"][j], kern, hy["skip"][j], tabs,
                             seq_len=seq_len, tm=tm_seq, cc=DFT_ROWS, dd=DFT_COLS, bb=bb)
            w_out = hy["w_out"][j]
        else:
            qkv = _norm_matmul(h, mix_norm[i], at["w_qkv"][j], tm=tm_seq)
            z = _attention(qkv, at["q_norm"][j], at["k_norm"][j], at["sink"][j], band_bias, meta_bias,
                           seq_len=seq_len)
            w_out = at["w_out"][j]
        hf = _proj_residual(h.reshape(b * lp, d), z.reshape(b * lp, d), w_out, tm=tm_flat)
        hf = _mlp(hf, mlp_norm[i], mlp_w_up[i], mlp_w_down[i], tm=tm_flat)
        h = hf.reshape(b, lp, d)
    return h[:, N_META:seq_len]


def kernel(x_prompt, x_sample, meta_tokens, rel_bias, mix_norm, mlp_norm, hy_w_in, hy_conv_w, hy_conv_b, hy_f_w1, hy_f_b1, hy_f_freq1, hy_f_w2, hy_f_b2, hy_f_freq2, hy_f_w3, hy_skip, hy_w_out, at_w_qkv, at_q_norm, at_k_norm, at_sink, at_w_out, mlp_w_up, mlp_w_down):
    hy = dict(w_in=hy_w_in.astype(BF16), conv_w=hy_conv_w, conv_b=hy_conv_b, f_w1=hy_f_w1, f_b1=hy_f_b1,
              f_freq1=hy_f_freq1, f_w2=hy_f_w2, f_b2=hy_f_b2, f_freq2=hy_f_freq2, f_w3=hy_f_w3,
              skip=hy_skip, w_out=hy_w_out.astype(BF16))
    at = dict(w_qkv=at_w_qkv.astype(BF16), q_norm=at_q_norm, k_norm=at_k_norm, sink=at_sink,
              w_out=at_w_out.astype(BF16))
    w_up = mlp_w_up.astype(BF16)
    w_down = mlp_w_down.astype(BF16)
    run = functools.partial(_trunk, meta_tokens=meta_tokens, rel_bias=rel_bias, mix_norm=mix_norm,
                            mlp_norm=mlp_norm, hy=hy, at=at, mlp_w_up=w_up, mlp_w_down=w_down)
    return (run(x_prompt), run(x_sample))
```

```python
import functools
import math

import jax
import jax.numpy as jnp
from jax import lax
from jax.experimental import pallas as pl
from jax.experimental.pallas import tpu as pltpu

F32 = jnp.float32
BF16 = jnp.bfloat16

D_MODEL = 1024
N_META = 16
HYENA_ORDER = 2
FILTER_EMB = 33
FILTER_HIDDEN = 64
FILTER_BANDS = (FILTER_EMB - 1) // 2
DECAY_TARGET = 1e-2
FAST_DECAY_PCT = 0.3
SLOW_DECAY_PCT = 1.5
N_HEADS = 16
N_KV_HEADS = 4
HEAD_DIM = D_MODEL // N_HEADS
GROUP = N_HEADS // N_KV_HEADS
WINDOW = 128
BLOCK = 128
REL_BUCKETS = 32
REL_MAX_DIST = 128
D_FF = 4 * D_MODEL
EPS = 1e-6
NEG = -1e30

PAGE = 128
LANES = 128
FEAT_PAD = LANES
VMEM_LIMIT_BYTES = 52 * 1024 * 1024
SUB = 16
HALO = SUB
DFT_ROWS = 32
DFT_COLS = 512


def _cparams(*sem):
    return pltpu.CompilerParams(dimension_semantics=sem, vmem_limit_bytes=VMEM_LIMIT_BYTES)


def _rms(x, g):
    ms = jnp.mean(x * x, axis=-1, keepdims=True)
    return (x * lax.rsqrt(ms + EPS)) * g


def _norm_matmul_body(x_ref, g_ref, w_ref, o_ref, *, seq_len, tm):
    xn = _rms(x_ref[0], g_ref[...])
    y = jnp.dot(xn.astype(BF16), w_ref[...], preferred_element_type=F32)
    if seq_len is not None:
        row = pl.program_id(1) * tm + lax.broadcasted_iota(jnp.int32, (tm, 1), 0)
        y = jnp.where(row < seq_len, y, 0.0)
    o_ref[0] = y.astype(o_ref.dtype)


def _norm_matmul(h, g, w, *, tm, seq_len=None):
    b, lp, d = h.shape
    n = w.shape[1]
    return pl.pallas_call(
        functools.partial(_norm_matmul_body, seq_len=seq_len, tm=tm),
        grid=(b, lp // tm),
        in_specs=[
            pl.BlockSpec((1, tm, d), lambda i, j: (i, j, 0)),
            pl.BlockSpec((1, d), lambda i, j: (0, 0)),
            pl.BlockSpec((d, n), lambda i, j: (0, 0)),
        ],
        out_specs=pl.BlockSpec((1, tm, n), lambda i, j: (i, j, 0)),
        out_shape=jax.ShapeDtypeStruct((b, lp, n), BF16),
        compiler_params=_cparams("parallel", "parallel"),
        name="norm_matmul",
    )(h, g.reshape(1, d), w)


def _proj_residual_body(h_ref, z_ref, w_ref, o_ref):
    o_ref[...] = h_ref[...] + jnp.dot(z_ref[...], w_ref[...], preferred_element_type=F32)


def _proj_residual(h, z, w, *, tm):
    m, d = h.shape
    k = z.shape[1]
    return pl.pallas_call(
        _proj_residual_body,
        grid=(m // tm,),
        in_specs=[
            pl.BlockSpec((tm, d), lambda i: (i, 0)),
            pl.BlockSpec((tm, k), lambda i: (i, 0)),
            pl.BlockSpec((k, d), lambda i: (0, 0)),
        ],
        out_specs=pl.BlockSpec((tm, d), lambda i: (i, 0)),
        out_shape=jax.ShapeDtypeStruct((m, d), F32),
        compiler_params=_cparams("parallel"),
        name="proj_residual",
    )(h, z, w)


def _mlp_body(x_ref, g_ref, wu_ref, wd_ref, o_ref, *, f_chunk):
    x = x_ref[...]
    xb = _rms(x, g_ref[...]).astype(BF16)
    acc = x
    for f in range(D_FF // f_chunk):
        u = jnp.dot(xb, wu_ref[:, f * f_chunk:(f + 1) * f_chunk], preferred_element_type=F32)
        a = jnp.square(jnp.maximum(u, 0.0)).astype(BF16)
        acc = acc + jnp.dot(a, wd_ref[f * f_chunk:(f + 1) * f_chunk, :], preferred_element_type=F32)
    o_ref[...] = acc


def _mlp(h, g, w_up, w_down, *, tm, f_chunk=1024):
    m, d = h.shape
    return pl.pallas_call(
        functools.partial(_mlp_body, f_chunk=f_chunk),
        grid=(m // tm,),
        in_specs=[
            pl.BlockSpec((tm, d), lambda i: (i, 0)),
            pl.BlockSpec((1, d), lambda i: (0, 0)),
            pl.BlockSpec((d, D_FF), lambda i: (0, 0)),
            pl.BlockSpec((D_FF, d), lambda i: (0, 0)),
        ],
        out_specs=pl.BlockSpec((tm, d), lambda i: (i, 0)),
        out_shape=jax.ShapeDtypeStruct((m, d), F32),
        compiler_params=_cparams("parallel"),
        name="mlp",
    )(h, g.reshape(1, d), w_up, w_down)


def _short_conv_body(prev_ref, cur_ref, next_ref, w_ref, b_ref, o_ref, *, seq_len, tm):
    j = pl.program_id(1)
    x = cur_ref[0].astype(F32)
    prev_row = jnp.where(j > 0, prev_ref[0].astype(F32)[HALO - 1:HALO, :], 0.0)
    next_row = next_ref[0].astype(F32)[0:1, :]
    rid = lax.broadcasted_iota(jnp.int32, (tm, 1), 0)
    up = jnp.where(rid == 0, prev_row, pltpu.roll(x, 1, 0))
    dn = jnp.where(rid == tm - 1, next_row, pltpu.roll(x, tm - 1, 0))
    y = up * w_ref[0, 0:1, :] + x * w_ref[0, 1:2, :] + dn * w_ref[0, 2:3, :] + b_ref[0]
    y = jnp.where(j * tm + rid < seq_len, y, 0.0)
    o_ref[0, 0] = y.astype(o_ref.dtype)


def _short_conv(u, w, bias, *, tm, seq_len):
    b, lp, d3 = u.shape
    d = d3 // 3
    nh = tm // HALO
    last = lp // HALO - 1
    w3 = w.reshape(3, 3, d).transpose(1, 0, 2)
    b3 = bias.reshape(3, 1, d)
    return pl.pallas_call(
        functools.partial(_short_conv_body, seq_len=seq_len, tm=tm),
        grid=(b, lp // tm, 3),
        in_specs=[
            pl.BlockSpec((1, HALO, d), lambda i, j, c: (i, jnp.maximum(j * nh - 1, 0), c)),
            pl.BlockSpec((1, tm, d), lambda i, j, c: (i, j, c)),
            pl.BlockSpec((1, HALO, d), lambda i, j, c: (i, jnp.minimum((j + 1) * nh, last), c)),
            pl.BlockSpec((1, 3, d), lambda i, j, c: (c, 0, 0)),
            pl.BlockSpec((1, 1, d), lambda i, j, c: (c, 0, 0)),
        ],
        out_specs=pl.BlockSpec((1, 1, tm, d), lambda i, j, c: (c, i, j, 0)),
        out_shape=jax.ShapeDtypeStruct((3, b, lp, d), BF16),
        compiler_params=_cparams("parallel", "parallel", "parallel"),
        name="short_conv",
    )(u, u, u, w3, b3)


def _filter_body(z_ref, w1_ref, b1_ref, fr1_ref, w2_ref, b2_ref, fr2_ref, w3_ref, delta_ref, o_ref,
                 *, seq_len, n, tm):
    hp = lax.Precision.HIGHEST
    z = z_ref[...]
    h = jnp.sin(fr1_ref[...] * (jnp.dot(z, w1_ref[...], precision=hp, preferred_element_type=F32) + b1_ref[...]))
    h = jnp.sin(fr2_ref[...] * (jnp.dot(h, w2_ref[...], precision=hp, preferred_element_type=F32) + b2_ref[...]))
    k = jnp.dot(h, w3_ref[...], precision=hp, preferred_element_type=F32)
    decay = jnp.exp(-z[:, 0:1] * delta_ref[...])
    d = delta_ref.shape[1]
    p = pl.program_id(0) * tm + lax.broadcasted_iota(jnp.int32, (tm, 1), 0)
    fwd = p < seq_len
    bwd = p > n - seq_len
    for o in range(HYENA_ORDER):
        kf = k[:, (2 * o) * d:(2 * o + 1) * d]
        kb = k[:, (2 * o + 1) * d:(2 * o + 2) * d]
        o_ref[o] = (jnp.where(fwd, kf, jnp.where(bwd, kb, 0.0)) * decay).astype(o_ref.dtype)


def _pad2(x, rows, cols):
    return jnp.pad(x, ((0, rows - x.shape[0]), (0, cols - x.shape[1])))


def _hyena_kernels(seq_len, n, f_w1, f_b1, f_freq1, f_w2, f_b2, f_freq2, f_w3):
    d = D_MODEL
    tm = _seq_tile(n)
    p = jnp.arange(n, dtype=jnp.int32)
    lag = jnp.where(p < seq_len, p, jnp.where(p > n - seq_len, n - p, 0))
    t = jnp.linspace(0.0, 1.0, seq_len, dtype=F32)[lag][:, None]
    w = ((2.0 * math.pi / seq_len) * jnp.arange(seq_len, dtype=F32))[lag][:, None]
    f = jnp.linspace(1e-4, FILTER_BANDS - 1, FILTER_BANDS, dtype=F32)[None, :]
    z = _pad2(jnp.concatenate([t, jnp.cos(f * w), -jnp.sin(f * w)], axis=-1), n, FEAT_PAD)
    max_decay = math.log(DECAY_TARGET) / FAST_DECAY_PCT
    min_decay = math.log(DECAY_TARGET) / SLOW_DECAY_PCT
    deltas = jnp.abs(jnp.linspace(min_decay, max_decay, d, dtype=F32))[None, :]
    hpad = FEAT_PAD
    args = (
        z,
        _pad2(f_w1, FEAT_PAD, hpad), _pad2(f_b1[None, :], 1, hpad), _pad2(f_freq1[None, :], 1, hpad),
        _pad2(f_w2, hpad, hpad), _pad2(f_b2[None, :], 1, hpad), _pad2(f_freq2[None, :], 1, hpad),
        _pad2(f_w3, hpad, f_w3.shape[1]), deltas,
    )
    const = lambda i: (0, 0)
    return pl.pallas_call(
        functools.partial(_filter_body, seq_len=seq_len, n=n, tm=tm),
        grid=(n // tm,),
        in_specs=[pl.BlockSpec((tm, FEAT_PAD), lambda i: (i, 0))]
        + [pl.BlockSpec(a.shape, const) for a in args[1:]],
        out_specs=pl.BlockSpec((HYENA_ORDER, tm, d), lambda i: (0, i, 0)),
        out_shape=jax.ShapeDtypeStruct((HYENA_ORDER, n, d), BF16),
        compiler_params=_cparams("parallel"),
        name="hyena_filter",
    )(*args)


def _angle(num, den):
    return (2.0 * math.pi / den) * (num % den).astype(F32)


def _dft_tables(pages):
    n1 = 2 * pages - 1
    n = n1 * PAGE
    k1 = jnp.arange(pages, dtype=jnp.int32)
    eye = jnp.eye(SUB, dtype=F32)

    def stage1(n_in):
        a = jnp.arange(n_in, dtype=jnp.int32)
        ang = _angle(k1[:, None] * a[None, :], n1)
        return jnp.kron(jnp.concatenate([jnp.cos(ang), -jnp.sin(ang)], axis=0), eye).astype(BF16)
    g_data = stage1(pages)
    g_filt = stage1(n1)
    c = jnp.arange(PAGE, dtype=jnp.int32)
    k = k1[:, None, None] + n1 * c[None, :, None]
    ang = _angle(k * c[None, None, :], n)
    cs, sn = jnp.cos(ang), jnp.sin(ang)
    t_fwd = jnp.concatenate([jnp.concatenate([cs, sn], axis=2),
                             jnp.concatenate([-sn, cs], axis=2)], axis=1).astype(BF16)
    cst, snt = cs.transpose(0, 2, 1), sn.transpose(0, 2, 1)
    t_inv = jnp.concatenate([jnp.concatenate([cst, -snt], axis=2),
                             jnp.concatenate([snt, cst], axis=2)], axis=1).astype(BF16)
    a = jnp.arange(pages, dtype=jnp.int32)
    ang = _angle(a[:, None] * k1[None, :], n1)
    coef = jnp.where(k1 == 0, 1.0, 2.0)[None, :] / n
    g_inv = jnp.kron(jnp.concatenate([coef * jnp.cos(ang), -coef * jnp.sin(ang)], axis=1), eye).astype(BF16)
    return dict(g_data=g_data, g_filt=g_filt, t_fwd=t_fwd, t_inv=t_inv, g_inv=g_inv, n=n, n1=n1)


def _dft1_body(g_ref, z_ref, a_ref, *, pin, pout, cc, dd):
    for j in range(cc // SUB):
        rows = slice(j * SUB, (j + 1) * SUB)
        x = z_ref[0, :, rows, :].reshape(pin * SUB, dd)
        r = jnp.dot(g_ref[...], x, preferred_element_type=F32)
        a_ref[0, :, :, rows, :] = r.reshape(2, pout, SUB, dd).astype(a_ref.dtype)


def _dft1(g, z, *, cc, dd):
    b, pin, _, d = z.shape
    pout = g.shape[0] // (2 * SUB)
    return pl.pallas_call(
        functools.partial(_dft1_body, pin=pin, pout=pout, cc=cc, dd=dd),
        grid=(b, PAGE // cc, d // dd),
        in_specs=[
            pl.BlockSpec(g.shape, lambda i, j, k: (0, 0)),
            pl.BlockSpec((1, pin, cc, dd), lambda i, j, k: (i, 0, j, k)),
        ],
        out_specs=pl.BlockSpec((1, 2, pout, cc, dd), lambda i, j, k: (i, 0, 0, j, k)),
        out_shape=jax.ShapeDtypeStruct((b, 2, pout, PAGE, d), BF16),
        compiler_params=_cparams("parallel", "parallel", "parallel"),
        name="dft_stage1",
    )(g, z)


def _filter_spectrum_body(t_ref, a_ref, o_ref):
    a = jnp.concatenate([a_ref[0, 0, 0], a_ref[0, 1, 0]], axis=0)
    x = jnp.dot(t_ref[0], a, preferred_element_type=F32)
    o_ref[0, 0, 0] = x[:PAGE]
    o_ref[0, 1, 0] = x[PAGE:]


def _filter_spectrum(t_fwd, af):
    o, _, pages, _, d = af.shape
    return pl.pallas_call(
        _filter_spectrum_body,
        grid=(pages, o),
        in_specs=[
            pl.BlockSpec((1, 2 * PAGE, 2 * PAGE), lambda k, i: (k, 0, 0)),
            pl.BlockSpec((1, 2, 1, PAGE, d), lambda k, i: (i, 0, k, 0, 0)),
        ],
        out_specs=pl.BlockSpec((1, 2, 1, PAGE, d), lambda k, i: (i, 0, k, 0, 0)),
        out_shape=jax.ShapeDtypeStruct(af.shape, F32),
        compiler_params=_cparams("parallel", "parallel"),
        name="filter_spectrum",
    )(t_fwd, af)


def _spectral_body(t_ref, ti_ref, kf_ref, a_ref, o_ref, *, bb):
    kr = kf_ref[0, 0]
    ki = kf_ref[1, 0]
    for b in range(bb):
        a = jnp.concatenate([a_ref[b, 0, 0], a_ref[b, 1, 0]], axis=0)
        x = jnp.dot(t_ref[0], a, preferred_element_type=F32)
        xr, xi = x[:PAGE], x[PAGE:]
        y = jnp.concatenate([xr * kr - xi * ki, xr * ki + xi * kr], axis=0).astype(BF16)
        r = jnp.dot(ti_ref[0], y, preferred_element_type=F32)
        o_ref[b, 0, 0] = r[:PAGE].astype(o_ref.dtype)
        o_ref[b, 1, 0] = r[PAGE:].astype(o_ref.dtype)


def _spectral(t_fwd, t_inv, kf, a5, *, bb):
    b, _, pages, _, d = a5.shape
    return pl.pallas_call(
        functools.partial(_spectral_body, bb=bb),
        grid=(pages, b // bb),
        in_specs=[
            pl.BlockSpec((1, 2 * PAGE, 2 * PAGE), lambda k, i: (k, 0, 0)),
            pl.BlockSpec((1, 2 * PAGE, 2 * PAGE), lambda k, i: (k, 0, 0)),
            pl.BlockSpec((2, 1, PAGE, d), lambda k, i: (0, k, 0, 0)),
            pl.BlockSpec((bb, 2, 1, PAGE, d), lambda k, i: (i, 0, k, 0, 0)),
        ],
        out_specs=pl.BlockSpec((bb, 2, 1, PAGE, d), lambda k, i: (i, 0, k, 0, 0)),
        out_shape=jax.ShapeDtypeStruct(a5.shape, BF16),
        compiler_params=_cparams("parallel", "parallel"),
        name="spectral_stage2",
    )(t_fwd, t_inv, kf, a5)


def _idft1_body(gi_ref, b_ref, z_ref, gate_ref, skip_ref, o_ref, *, seq_len, pages, cc, dd):
    base = (lax.broadcasted_iota(jnp.int32, (pages, SUB, 1), 0) * PAGE + pl.program_id(1) * cc
            + lax.broadcasted_iota(jnp.int32, (pages, SUB, 1), 1))
    for j in range(cc // SUB):
        rows = slice(j * SUB, (j + 1) * SUB)
        x = b_ref[0, :, :, rows, :].reshape(2 * pages * SUB, dd)
        y = jnp.dot(gi_ref[...], x, preferred_element_type=F32).reshape(pages, SUB, dd)
        out = gate_ref[0, :, rows, :].astype(F32) * (y + skip_ref[...] * z_ref[0, :, rows, :].astype(F32))
        o_ref[0, :, rows, :] = jnp.where(base + j * SUB < seq_len, out, 0.0).astype(o_ref.dtype)


def _idft1(g_inv, bint, z, gate, skip, *, cc, dd, seq_len):
    b, pages, _, d = z.shape
    return pl.pallas_call(
        functools.partial(_idft1_body, seq_len=seq_len, pages=pages, cc=cc, dd=dd),
        grid=(b, PAGE // cc, d // dd),
        in_specs=[
            pl.BlockSpec(g_inv.shape, lambda i, j, k: (0, 0)),
            pl.BlockSpec((1, 2, pages, cc, dd), lambda i, j, k: (i, 0, 0, j, k)),
            pl.BlockSpec((1, pages, cc, dd), lambda i, j, k: (i, 0, j, k)),
            pl.BlockSpec((1, pages, cc, dd), lambda i, j, k: (i, 0, j, k)),
            pl.BlockSpec((1, dd), lambda i, j, k: (0, k)),
        ],
        out_specs=pl.BlockSpec((1, pages, cc, dd), lambda i, j, k: (i, 0, j, k)),
        out_shape=jax.ShapeDtypeStruct(z.shape, BF16),
        compiler_params=_cparams("parallel", "parallel", "parallel"),
        name="idft_stage1",
    )(g_inv, bint, z, gate, skip.astype(F32).reshape(1, d))


def _hyena_mixer(hn_proj, conv_w, conv_b, kern, skip, tabs, *, seq_len, tm, cc, dd, bb):
    b, lp, _ = hn_proj.shape
    d = D_MODEL
    pages = lp // PAGE
    x1, x2, v = (t.reshape(b, pages, PAGE, d) for t in _short_conv(hn_proj, conv_w, conv_b, tm=tm, seq_len=seq_len))
    af = _dft1(tabs["g_filt"], kern.reshape(HYENA_ORDER, tabs["n1"], PAGE, d), cc=cc, dd=dd)
    kf = _filter_spectrum(tabs["t_fwd"], af)
    z = v
    for o, gate in enumerate((x1, x2)):
        a = _dft1(tabs["g_data"], z, cc=cc, dd=dd)
        r = _spectral(tabs["t_fwd"], tabs["t_inv"], kf[o], a, bb=bb)
        z = _idft1(tabs["g_inv"], r, z, gate, skip[o], cc=cc, dd=dd, seq_len=seq_len)
    return z.reshape(b, lp, d)


def _t5_bucket(rel):
    half = REL_BUCKETS // 2
    max_exact = half // 2
    n = jnp.abs(rel)
    large = max_exact + (jnp.log(jnp.maximum(n, 1).astype(F32) / max_exact)
                         / math.log(REL_MAX_DIST / max_exact) * (half - max_exact)).astype(jnp.int32)
    large = jnp.minimum(large, half - 1)
    return jnp.where(rel > 0, half, 0) + jnp.where(n < max_exact, n, large)


def _bias_table_body(rb_ref, bucket_ref, o_ref):
    bk = bucket_ref[...]
    for h in range(N_HEADS):
        acc = jnp.where(bk < 0, NEG, 0.0).astype(F32)
        for r in range(REL_BUCKETS):
            acc = jnp.where(bk == r, rb_ref[r, h], acc)
        o_ref[h] = acc


def _bias_table(rel_bias, bucket, *, tr):
    r, c = bucket.shape
    return pl.pallas_call(
        _bias_table_body,
        grid=(r // tr,),
        in_specs=[
            pl.BlockSpec(memory_space=pltpu.SMEM),
            pl.BlockSpec((tr, c), lambda i: (i, 0)),
        ],
        out_specs=pl.BlockSpec((N_HEADS, tr, c), lambda i: (0, i, 0)),
        out_shape=jax.ShapeDtypeStruct((N_HEADS, r, c), F32),
        compiler_params=_cparams("parallel"),
        name="bias_table",
    )(rel_bias.astype(F32), bucket)


def _attn_body(sink_ref, q_ref, kp_ref, kc_ref, kn_ref, vp_ref, vc_ref, vn_ref, km_ref, vm_ref,
               qg_ref, kg_ref, bb_ref, mb_ref, o_ref, *, seq_len, nb):
    i = pl.program_id(1)
    hd = HEAD_DIM

    def headnorm(x, g):
        ms = jnp.mean(x * x, axis=-1, keepdims=True)
        return x * lax.rsqrt(ms + EPS) * g

    kpos = (i - 1) * BLOCK + lax.broadcasted_iota(jnp.int32, (1, 3 * BLOCK), 1)
    kvalid = (kpos >= N_META) & (kpos < seq_len)
    qg = qg_ref[...] * (hd ** -0.5)
    kg = kg_ref[...]
    nk = 3 * BLOCK + N_META
    ones = jnp.ones((nk, hd), BF16)
    for g in range(N_KV_HEADS):
        ks = slice(g * hd, (g + 1) * hd)
        hs = slice(g * GROUP, (g + 1) * GROUP)
        k_all = jnp.concatenate([kp_ref[0, :, ks], kc_ref[0, :, ks], kn_ref[0, :, ks], km_ref[0, :, ks]], axis=0)
        k_all = headnorm(k_all.astype(F32), kg).astype(BF16)
        v_all = jnp.concatenate([vp_ref[0, :, ks], vc_ref[0, :, ks], vn_ref[0, :, ks], vm_ref[0, :, ks]], axis=0)
        v_ext = jnp.concatenate([v_all, ones], axis=1)
        q = jnp.concatenate([q_ref[0, :, (g * GROUP + j) * hd:(g * GROUP + j + 1) * hd] for j in range(GROUP)],
                            axis=0)
        q = headnorm(q.astype(F32), qg).astype(BF16)
        bias = jnp.concatenate(
            [jnp.where(kvalid, bb_ref[hs].reshape(GROUP * BLOCK, 3 * BLOCK), NEG),
             mb_ref[hs].reshape(GROUP * BLOCK, N_META)], axis=1)
        s = lax.dot_general(q, k_all, (((1,), (1,)), ((), ())), preferred_element_type=F32) + bias
        sink = jnp.concatenate([jnp.full((BLOCK, 1), sink_ref[g * GROUP + j], F32) for j in range(GROUP)], axis=0)
        m = jnp.maximum(jnp.max(s, axis=-1, keepdims=True), sink)
        p = jnp.exp(s - m).astype(BF16)
        r = jnp.dot(p, v_ext, preferred_element_type=F32)
        o = r[:, :hd] / (r[:, hd:] + jnp.exp(sink - m))
        for j in range(GROUP):
            h = g * GROUP + j
            o_ref[0, :, h * hd:(h + 1) * hd] = o[j * BLOCK:(j + 1) * BLOCK].astype(o_ref.dtype)


def _attention(qkv, q_gain, k_gain, sink, band_bias, meta_bias, *, seq_len):
    b, lp, _ = qkv.shape
    nb = lp // BLOCK
    dq = N_HEADS * HEAD_DIM
    dkv = N_KV_HEADS * HEAD_DIM
    kcol = dq // dkv
    vcol = kcol + 1
    prv = lambda i, j: jnp.maximum(j - 1, 0)
    nxt = lambda i, j: jnp.minimum(j + 1, nb - 1)
    blk = lambda rows, col: pl.BlockSpec((1, BLOCK, dkv), lambda i, j: (i, rows(i, j), col))
    cur = lambda i, j: j
    return pl.pallas_call(
        functools.partial(_attn_body, seq_len=seq_len, nb=nb),
        grid=(b, nb),
        in_specs=[
            pl.BlockSpec(memory_space=pltpu.SMEM),
            pl.BlockSpec((1, BLOCK, dq), lambda i, j: (i, j, 0)),
            blk(prv, kcol), blk(cur, kcol), blk(nxt, kcol),
            blk(prv, vcol), blk(cur, vcol), blk(nxt, vcol),
            pl.BlockSpec((1, N_META, dkv), lambda i, j: (i, 0, kcol)),
            pl.BlockSpec((1, N_META, dkv), lambda i, j: (i, 0, vcol)),
            pl.BlockSpec((1, HEAD_DIM), lambda i, j: (0, 0)),
            pl.BlockSpec((1, HEAD_DIM), lambda i, j: (0, 0)),
            pl.BlockSpec((N_HEADS, BLOCK, 3 * BLOCK), lambda i, j: (0, 0, 0)),
            pl.BlockSpec((N_HEADS, BLOCK, N_META), lambda i, j: (0, j, 0)),
        ],
        out_specs=pl.BlockSpec((1, BLOCK, dq), lambda i, j: (i, j, 0)),
        out_shape=jax.ShapeDtypeStruct((b, lp, dq), BF16),
        compiler_params=_cparams("parallel", "parallel"),
        name="window_attention",
    )(sink.astype(F32), qkv, qkv, qkv, qkv, qkv, qkv, qkv, qkv, qkv,
      q_gain.reshape(1, HEAD_DIM).astype(F32), k_gain.reshape(1, HEAD_DIM).astype(F32), band_bias, meta_bias)


def _attention_bias_tables(rel_bias, lp):
    r = jnp.arange(BLOCK, dtype=jnp.int32)[:, None]
    s = jnp.arange(3 * BLOCK, dtype=jnp.int32)[None, :]
    rel = s - BLOCK - r
    band_bucket = jnp.where(jnp.abs(rel) <= WINDOW, _t5_bucket(rel), -1)
    qpos = jnp.arange(lp, dtype=jnp.int32)[:, None]
    meta_bucket = _t5_bucket(jnp.arange(N_META, dtype=jnp.int32)[None, :] - qpos)
    return (_bias_table(rel_bias, band_bucket, tr=BLOCK), _bias_table(rel_bias, meta_bucket, tr=_seq_tile(lp)))


def _seq_tile(lp):
    pages = lp // PAGE
    for k in (5, 4, 3, 2, 1):
        if pages % k == 0:
            return k * PAGE
    return PAGE


def _trunk(x, meta_tokens, rel_bias, mix_norm, mlp_norm, hy, at, mlp_w_up, mlp_w_down):
    b, s, d = x.shape
    seq_len = N_META + s
    pages = -(-seq_len // PAGE)
    lp = pages * PAGE
    assert seq_len <= lp - 64
    tm_seq = _seq_tile(lp)
    tm_flat = 512
    assert (b * lp) % tm_flat == 0
    bb = 4

    meta = jnp.broadcast_to(meta_tokens[None].astype(x.dtype), (b, N_META, d))
    h = jnp.concatenate([meta, x, jnp.zeros((b, lp - seq_len, d), x.dtype)], axis=1)

    tabs = _dft_tables(pages)
    band_bias, meta_bias = _attention_bias_tables(rel_bias, lp)

    depth = mix_norm.shape[0]
    for i in range(depth):
        j = i // 2
        if i % 2 == 0:
            u = _norm_matmul(h, mix_norm[i], hy["w_in"][j], tm=tm_seq, seq_len=seq_len)
            kern = _hyena_kernels(seq_len, tabs["n"], hy["f_w1"][j], hy["f_b1"][j], hy["f_freq1"][j], hy["f_w2"][j],
                                  hy["f_b2"][j], hy["f_freq2"][j], hy["f_w3"][j])
            z = _hyena_mixer(u, hy["conv_w"][j], hy["conv_b"][j], kern, hy["skip"][j], tabs,
                             seq_len=seq_len, tm=tm_seq, cc=DFT_ROWS, dd=DFT_COLS, bb=bb)
            w_out = hy["w_out"][j]
        else:
            qkv = _norm_matmul(h, mix_norm[i], at["w_qkv"][j], tm=tm_seq)
            z = _attention(qkv, at["q_norm"][j], at["k_norm"][j], at["sink"][j], band_bias, meta_bias,
                           seq_len=seq_len)
            w_out = at["w_out"][j]
        hf = _proj_residual(h.reshape(b * lp, d), z.reshape(b * lp, d), w_out, tm=tm_flat)
        hf = _mlp(hf, mlp_norm[i], mlp_w_up[i], mlp_w_down[i], tm=tm_flat)
        h = hf.reshape(b, lp, d)
    return h[:, N_META:seq_len]


def kernel(x_prompt, x_sample, meta_tokens, rel_bias, mix_norm, mlp_norm, hy_w_in, hy_conv_w, hy_conv_b, hy_f_w1, hy_f_b1, hy_f_freq1, hy_f_w2, hy_f_b2, hy_f_freq2, hy_f_w3, hy_skip, hy_w_out, at_w_qkv, at_q_norm, at_k_norm, at_sink, at_w_out, mlp_w_up, mlp_w_down):
    hy = dict(w_in=hy_w_in.astype(BF16), conv_w=hy_conv_w, conv_b=hy_conv_b, f_w1=hy_f_w1, f_b1=hy_f_b1,
              f_freq1=hy_f_freq1, f_w2=hy_f_w2, f_b2=hy_f_b2, f_freq2=hy_f_freq2, f_w3=hy_f_w3,
              skip=hy_skip, w_out=hy_w_out.astype(BF16))
    at = dict(w_qkv=at_w_qkv.astype(BF16), q_norm=at_q_norm, k_norm=at_k_norm, sink=at_sink,
              w_out=at_w_out.astype(BF16))
    w_up = mlp_w_up.astype(BF16)
    w_down = mlp_w_down.astype(BF16)
    run = functools.partial(_trunk, meta_tokens=meta_tokens, rel_bias=rel_bias, mix_norm=mix_norm,
                            mlp_norm=mlp_norm, hy=hy, at=at, mlp_w_up=w_up, mlp_w_down=w_down)
    return (run(x_prompt), run(x_sample))
```

```python
import functools
import math

import jax
import jax.numpy as jnp
from jax import lax
from jax.experimental import pallas as pl
from jax.experimental.pallas import tpu as pltpu

F32 = jnp.float32
BF16 = jnp.bfloat16

D_MODEL = 1024
N_META = 16
HYENA_ORDER = 2
FILTER_EMB = 33
FILTER_HIDDEN = 64
FILTER_BANDS = (FILTER_EMB - 1) // 2
DECAY_TARGET = 1e-2
FAST_DECAY_PCT = 0.3
SLOW_DECAY_PCT = 1.5
N_HEADS = 16
N_KV_HEADS = 4
HEAD_DIM = D_MODEL // N_HEADS
GROUP = N_HEADS // N_KV_HEADS
WINDOW = 128
BLOCK = 128
REL_BUCKETS = 32
REL_MAX_DIST = 128
D_FF = 4 * D_MODEL
EPS = 1e-6
NEG = -1e30

PAGE = 128
LANES = 128
FEAT_PAD = LANES
VMEM_LIMIT_BYTES = 52 * 1024 * 1024
SUB = 16
HALO = SUB
DFT_ROWS = 32
DFT_COLS = 512


def _cparams(*sem):
    return pltpu.CompilerParams(dimension_semantics=sem, vmem_limit_bytes=VMEM_LIMIT_BYTES)


def _rms(x, g):
    ms = jnp.mean(x * x, axis=-1, keepdims=True)
    return (x * lax.rsqrt(ms + EPS)) * g


def _norm_matmul_body(x_ref, g_ref, w_ref, o_ref, *, seq_len, tm):
    xn = _rms(x_ref[0], g_ref[...])
    y = jnp.dot(xn.astype(BF16), w_ref[...], preferred_element_type=F32)
    if seq_len is not None:
        row = pl.program_id(1) * tm + lax.broadcasted_iota(jnp.int32, (tm, 1), 0)
        y = jnp.where(row < seq_len, y, 0.0)
    o_ref[0] = y.astype(o_ref.dtype)


def _norm_matmul(h, g, w, *, tm, seq_len=None):
    b, lp, d = h.shape
    n = w.shape[1]
    return pl.pallas_call(
        functools.partial(_norm_matmul_body, seq_len=seq_len, tm=tm),
        grid=(b, lp // tm),
        in_specs=[
            pl.BlockSpec((1, tm, d), lambda i, j: (i, j, 0)),
            pl.BlockSpec((1, d), lambda i, j: (0, 0)),
            pl.BlockSpec((d, n), lambda i, j: (0, 0)),
        ],
        out_specs=pl.BlockSpec((1, tm, n), lambda i, j: (i, j, 0)),
        out_shape=jax.ShapeDtypeStruct((b, lp, n), BF16),
        compiler_params=_cparams("parallel", "parallel"),
        name="norm_matmul",
    )(h, g.reshape(1, d), w)


def _qkv_body(x_ref, g_ref, w_ref, gain_ref, o_ref):
    xn = _rms(x_ref[0], g_ref[...])
    y = jnp.dot(xn.astype(BF16), w_ref[...], preferred_element_type=F32)
    nqk = (N_HEADS + N_KV_HEADS) * HEAD_DIM
    low = lax.broadcasted_iota(jnp.int32, (1, LANES), 1) < HEAD_DIM
    for p in range(nqk // LANES):
        cols = slice(p * LANES, (p + 1) * LANES)
        y2 = y[:, cols]
        sq = y2 * y2
        ms_lo = jnp.sum(jnp.where(low, sq, 0.0), axis=-1, keepdims=True) * (1.0 / HEAD_DIM)
        ms_hi = jnp.sum(jnp.where(low, 0.0, sq), axis=-1, keepdims=True) * (1.0 / HEAD_DIM)
        r = jnp.where(low, lax.rsqrt(ms_lo + EPS), lax.rsqrt(ms_hi + EPS))
        o_ref[0, :, cols] = (y2 * r * gain_ref[:, cols]).astype(o_ref.dtype)
    o_ref[0, :, nqk:] = y[:, nqk:].astype(o_ref.dtype)


def _qkv_proj(h, g, w, q_gain, k_gain, *, tm):
    b, lp, d = h.shape
    n = w.shape[1]
    gain = jnp.concatenate([jnp.tile(q_gain.astype(F32) * (HEAD_DIM ** -0.5), N_HEADS),
                            jnp.tile(k_gain.astype(F32), N_KV_HEADS)])[None, :]
    return pl.pallas_call(
        _qkv_body,
        grid=(b, lp // tm),
        in_specs=[
            pl.BlockSpec((1, tm, d), lambda i, j: (i, j, 0)),
            pl.BlockSpec((1, d), lambda i, j: (0, 0)),
            pl.BlockSpec((d, n), lambda i, j: (0, 0)),
            pl.BlockSpec(gain.shape, lambda i, j: (0, 0)),
        ],
        out_specs=pl.BlockSpec((1, tm, n), lambda i, j: (i, j, 0)),
        out_shape=jax.ShapeDtypeStruct((b, lp, n), BF16),
        compiler_params=_cparams("parallel", "parallel"),
        name="qkv_proj",
    )(h, g.reshape(1, d), w, gain)


def _proj_residual_body(h_ref, z_ref, w_ref, o_ref):
    o_ref[...] = h_ref[...] + jnp.dot(z_ref[...], w_ref[...], preferred_element_type=F32)


def _proj_residual(h, z, w, *, tm):
    m, d = h.shape
    k = z.shape[1]
    return pl.pallas_call(
        _proj_residual_body,
        grid=(m // tm,),
        in_specs=[
            pl.BlockSpec((tm, d), lambda i: (i, 0)),
            pl.BlockSpec((tm, k), lambda i: (i, 0)),
            pl.BlockSpec((k, d), lambda i: (0, 0)),
        ],
        out_specs=pl.BlockSpec((tm, d), lambda i: (i, 0)),
        out_shape=jax.ShapeDtypeStruct((m, d), F32),
        compiler_params=_cparams("parallel"),
        name="proj_residual",
    )(h, z, w)


def _mlp_body(x_ref, g_ref, wu_ref, wd_ref, o_ref, *, f_chunk):
    x = x_ref[...]
    xb = _rms(x, g_ref[...]).astype(BF16)
    acc = x
    for f in range(D_FF // f_chunk):
        u = jnp.dot(xb, wu_ref[:, f * f_chunk:(f + 1) * f_chunk], preferred_element_type=F32)
        a = jnp.square(jnp.maximum(u, 0.0)).astype(BF16)
        acc = acc + jnp.dot(a, wd_ref[f * f_chunk:(f + 1) * f_chunk, :], preferred_element_type=F32)
    o_ref[...] = acc


def _mlp(h, g, w_up, w_down, *, tm, f_chunk=1024):
    m, d = h.shape
    return pl.pallas_call(
        functools.partial(_mlp_body, f_chunk=f_chunk),
        grid=(m // tm,),
        in_specs=[
            pl.BlockSpec((tm, d), lambda i: (i, 0)),
            pl.BlockSpec((1, d), lambda i: (0, 0)),
            pl.BlockSpec((d, D_FF), lambda i: (0, 0)),
            pl.BlockSpec((D_FF, d), lambda i: (0, 0)),
        ],
        out_specs=pl.BlockSpec((tm, d), lambda i: (i, 0)),
        out_shape=jax.ShapeDtypeStruct((m, d), F32),
        compiler_params=_cparams("parallel"),
        name="mlp",
    )(h, g.reshape(1, d), w_up, w_down)


def _short_conv_body(prev_ref, cur_ref, next_ref, w_ref, b_ref, o_ref, *, seq_len, tm):
    j = pl.program_id(1)
    x = cur_ref[0].astype(F32)
    prev_row = jnp.where(j > 0, prev_ref[0].astype(F32)[HALO - 1:HALO, :], 0.0)
    next_row = next_ref[0].astype(F32)[0:1, :]
    rid = lax.broadcasted_iota(jnp.int32, (tm, 1), 0)
    up = jnp.where(rid == 0, prev_row, pltpu.roll(x, 1, 0))
    dn = jnp.where(rid == tm - 1, next_row, pltpu.roll(x, tm - 1, 0))
    y = up * w_ref[0, 0:1, :] + x * w_ref[0, 1:2, :] + dn * w_ref[0, 2:3, :] + b_ref[0]
    y = jnp.where(j * tm + rid < seq_len, y, 0.0)
    o_ref[0, 0] = y.astype(o_ref.dtype)


def _short_conv(u, w, bias, *, tm, seq_len):
    b, lp, d3 = u.shape
    d = d3 // 3
    nh = tm // HALO
    last = lp // HALO - 1
    w3 = w.reshape(3, 3, d).transpose(1, 0, 2)
    b3 = bias.reshape(3, 1, d)
    return pl.pallas_call(
        functools.partial(_short_conv_body, seq_len=seq_len, tm=tm),
        grid=(b, lp // tm, 3),
        in_specs=[
            pl.BlockSpec((1, HALO, d), lambda i, j, c: (i, jnp.maximum(j * nh - 1, 0), c)),
            pl.BlockSpec((1, tm, d), lambda i, j, c: (i, j, c)),
            pl.BlockSpec((1, HALO, d), lambda i, j, c: (i, jnp.minimum((j + 1) * nh, last), c)),
            pl.BlockSpec((1, 3, d), lambda i, j, c: (c, 0, 0)),
            pl.BlockSpec((1, 1, d), lambda i, j, c: (c, 0, 0)),
        ],
        out_specs=pl.BlockSpec((1, 1, tm, d), lambda i, j, c: (c, i, j, 0)),
        out_shape=jax.ShapeDtypeStruct((3, b, lp, d), BF16),
        compiler_params=_cparams("parallel", "parallel", "parallel"),
        name="short_conv",
    )(u, u, u, w3, b3)


def _filter_body(z_ref, w1_ref, b1_ref, fr1_ref, w2_ref, b2_ref, fr2_ref, w3_ref, delta_ref, o_ref,
                 *, seq_len, n, tm):
    hp = lax.Precision.HIGHEST
    z = z_ref[...]
    h = jnp.sin(fr1_ref[...] * (jnp.dot(z, w1_ref[...], precision=hp, preferred_element_type=F32) + b1_ref[...]))
    h = jnp.sin(fr2_ref[...] * (jnp.dot(h, w2_ref[...], precision=hp, preferred_element_type=F32) + b2_ref[...]))
    k = jnp.dot(h, w3_ref[...], precision=hp, preferred_element_type=F32)
    decay = jnp.exp(-z[:, 0:1] * delta_ref[...])
    d = delta_ref.shape[1]
    p = pl.program_id(0) * tm + lax.broadcasted_iota(jnp.int32, (tm, 1), 0)
    fwd = p < seq_len
    bwd = p > n - seq_len
    for o in range(HYENA_ORDER):
        kf = k[:, (2 * o) * d:(2 * o + 1) * d]
        kb = k[:, (2 * o + 1) * d:(2 * o + 2) * d]
        o_ref[o] = (jnp.where(fwd, kf, jnp.where(bwd, kb, 0.0)) * decay).astype(o_ref.dtype)


def _pad2(x, rows, cols):
    return jnp.pad(x, ((0, rows - x.shape[0]), (0, cols - x.shape[1])))


def _hyena_kernels(seq_len, n, f_w1, f_b1, f_freq1, f_w2, f_b2, f_freq2, f_w3):
    d = D_MODEL
    tm = _seq_tile(n)
    p = jnp.arange(n, dtype=jnp.int32)
    lag = jnp.where(p < seq_len, p, jnp.where(p > n - seq_len, n - p, 0))
    t = jnp.linspace(0.0, 1.0, seq_len, dtype=F32)[lag][:, None]
    w = ((2.0 * math.pi / seq_len) * jnp.arange(seq_len, dtype=F32))[lag][:, None]
    f = jnp.linspace(1e-4, FILTER_BANDS - 1, FILTER_BANDS, dtype=F32)[None, :]
    z = _pad2(jnp.concatenate([t, jnp.cos(f * w), -jnp.sin(f * w)], axis=-1), n, FEAT_PAD)
    max_decay = math.log(DECAY_TARGET) / FAST_DECAY_PCT
    min_decay = math.log(DECAY_TARGET) / SLOW_DECAY_PCT
    deltas = jnp.abs(jnp.linspace(min_decay, max_decay, d, dtype=F32))[None, :]
    hpad = FEAT_PAD
    args = (
        z,
        _pad2(f_w1, FEAT_PAD, hpad), _pad2(f_b1[None, :], 1, hpad), _pad2(f_freq1[None, :], 1, hpad),
        _pad2(f_w2, hpad, hpad), _pad2(f_b2[None, :], 1, hpad), _pad2(f_freq2[None, :], 1, hpad),
        _pad2(f_w3, hpad, f_w3.shape[1]), deltas,
    )
    const = lambda i: (0, 0)
    return pl.pallas_call(
        functools.partial(_filter_body, seq_len=seq_len, n=n, tm=tm),
        grid=(n // tm,),
        in_specs=[pl.BlockSpec((tm, FEAT_PAD), lambda i: (i, 0))]
        + [pl.BlockSpec(a.shape, const) for a in args[1:]],
        out_specs=pl.BlockSpec((HYENA_ORDER, tm, d), lambda i: (0, i, 0)),
        out_shape=jax.ShapeDtypeStruct((HYENA_ORDER, n, d), BF16),
        compiler_params=_cparams("parallel"),
        name="hyena_filter",
    )(*args)


def _angle(num, den):
    return (2.0 * math.pi / den) * (num % den).astype(F32)


def _dft_tables(pages):
    n1 = 2 * pages - 1
    n = n1 * PAGE
    k1 = jnp.arange(pages, dtype=jnp.int32)
    eye = jnp.eye(SUB, dtype=F32)

    def stage1(n_in):
        a = jnp.arange(n_in, dtype=jnp.int32)
        ang = _angle(k1[:, None] * a[None, :], n1)
        return jnp.kron(jnp.concatenate([jnp.cos(ang), -jnp.sin(ang)], axis=0), eye).astype(BF16)
    g_data = stage1(pages)
    g_filt = stage1(n1)
    c = jnp.arange(PAGE, dtype=jnp.int32)
    k = k1[:, None, None] + n1 * c[None, :, None]
    ang = _angle(k * c[None, None, :], n)
    cs, sn = jnp.cos(ang), jnp.sin(ang)
    t_fwd = jnp.concatenate([jnp.concatenate([cs, sn], axis=2),
                             jnp.concatenate([-sn, cs], axis=2)], axis=1).astype(BF16)
    cst, snt = cs.transpose(0, 2, 1), sn.transpose(0, 2, 1)
    t_inv = jnp.concatenate([jnp.concatenate([cst, -snt], axis=2),
                             jnp.concatenate([snt, cst], axis=2)], axis=1).astype(BF16)
    a = jnp.arange(pages, dtype=jnp.int32)
    ang = _angle(a[:, None] * k1[None, :], n1)
    coef = jnp.where(k1 == 0, 1.0, 2.0)[None, :] / n
    g_inv = jnp.kron(jnp.concatenate([coef * jnp.cos(ang), -coef * jnp.sin(ang)], axis=1), eye).astype(BF16)
    return dict(g_data=g_data, g_filt=g_filt, t_fwd=t_fwd, t_inv=t_inv, g_inv=g_inv, n=n, n1=n1)


def _dft1_body(g_ref, z_ref, a_ref, *, pin, pout, cc, dd):
    for j in range(cc // SUB):
        rows = slice(j * SUB, (j + 1) * SUB)
        x = z_ref[0, :, rows, :].reshape(pin * SUB, dd)
        r = jnp.dot(g_ref[...], x, preferred_element_type=F32)
        a_ref[0, :, :, rows, :] = r.reshape(2, pout, SUB, dd).astype(a_ref.dtype)


def _dft1(g, z, *, cc, dd):
    b, pin, _, d = z.shape
    pout = g.shape[0] // (2 * SUB)
    return pl.pallas_call(
        functools.partial(_dft1_body, pin=pin, pout=pout, cc=cc, dd=dd),
        grid=(b, PAGE // cc, d // dd),
        in_specs=[
            pl.BlockSpec(g.shape, lambda i, j, k: (0, 0)),
            pl.BlockSpec((1, pin, cc, dd), lambda i, j, k: (i, 0, j, k)),
        ],
        out_specs=pl.BlockSpec((1, 2, pout, cc, dd), lambda i, j, k: (i, 0, 0, j, k)),
        out_shape=jax.ShapeDtypeStruct((b, 2, pout, PAGE, d), BF16),
        compiler_params=_cparams("parallel", "parallel", "parallel"),
        name="dft_stage1",
    )(g, z)


def _filter_spectrum_body(t_ref, a_ref, o_ref):
    a = jnp.concatenate([a_ref[0, 0, 0], a_ref[0, 1, 0]], axis=0)
    x = jnp.dot(t_ref[0], a, preferred_element_type=F32)
    o_ref[0, 0, 0] = x[:PAGE]
    o_ref[0, 1, 0] = x[PAGE:]


def _filter_spectrum(t_fwd, af):
    o, _, pages, _, d = af.shape
    return pl.pallas_call(
        _filter_spectrum_body,
        grid=(pages, o),
        in_specs=[
            pl.BlockSpec((1, 2 * PAGE, 2 * PAGE), lambda k, i: (k, 0, 0)),
            pl.BlockSpec((1, 2, 1, PAGE, d), lambda k, i: (i, 0, k, 0, 0)),
        ],
        out_specs=pl.BlockSpec((1, 2, 1, PAGE, d), lambda k, i: (i, 0, k, 0, 0)),
        out_shape=jax.ShapeDtypeStruct(af.shape, F32),
        compiler_params=_cparams("parallel", "parallel"),
        name="filter_spectrum",
    )(t_fwd, af)


def _spectral_body(t_ref, ti_ref, kf_ref, a_ref, o_ref, *, bb):
    kr = kf_ref[0, 0]
    ki = kf_ref[1, 0]
    for b in range(bb):
        a = jnp.concatenate([a_ref[b, 0, 0], a_ref[b, 1, 0]], axis=0)
        x = jnp.dot(t_ref[0], a, preferred_element_type=F32)
        xr, xi = x[:PAGE], x[PAGE:]
        y = jnp.concatenate([xr * kr - xi * ki, xr * ki + xi * kr], axis=0).astype(BF16)
        r = jnp.dot(ti_ref[0], y, preferred_element_type=F32)
        o_ref[b, 0, 0] = r[:PAGE].astype(o_ref.dtype)
        o_ref[b, 1, 0] = r[PAGE:].astype(o_ref.dtype)


def _spectral(t_fwd, t_inv, kf, a5, *, bb):
    b, _, pages, _, d = a5.shape
    return pl.pallas_call(
        functools.partial(_spectral_body, bb=bb),
        grid=(pages, b // bb),
        in_specs=[
            pl.BlockSpec((1, 2 * PAGE, 2 * PAGE), lambda k, i: (k, 0, 0)),
            pl.BlockSpec((1, 2 * PAGE, 2 * PAGE), lambda k, i: (k, 0, 0)),
            pl.BlockSpec((2, 1, PAGE, d), lambda k, i: (0, k, 0, 0)),
            pl.BlockSpec((bb, 2, 1, PAGE, d), lambda k, i: (i, 0, k, 0, 0)),
        ],
        out_specs=pl.BlockSpec((bb, 2, 1, PAGE, d), lambda k, i: (i, 0, k, 0, 0)),
        out_shape=jax.ShapeDtypeStruct(a5.shape, BF16),
        compiler_params=_cparams("parallel", "parallel"),
        name="spectral_stage2",
    )(t_fwd, t_inv, kf, a5)


def _idft1_body(gi_ref, b_ref, z_ref, gate_ref, skip_ref, o_ref, *, seq_len, pages, cc, dd):
    base = (lax.broadcasted_iota(jnp.int32, (pages, SUB, 1), 0) * PAGE + pl.program_id(1) * cc
            + lax.broadcasted_iota(jnp.int32, (pages, SUB, 1), 1))
    for j in range(cc // SUB):
        rows = slice(j * SUB, (j + 1) * SUB)
        x = b_ref[0, :, :, rows, :].reshape(2 * pages * SUB, dd)
        y = jnp.dot(gi_ref[...], x, preferred_element_type=F32).reshape(pages, SUB, dd)
        out = gate_ref[0, :, rows, :].astype(F32) * (y + skip_ref[...] * z_ref[0, :, rows, :].astype(F32))
        o_ref[0, :, rows, :] = jnp.where(base + j * SUB < seq_len, out, 0.0).astype(o_ref.dtype)


def _idft1(g_inv, bint, z, gate, skip, *, cc, dd, seq_len):
    b, pages, _, d = z.shape
    return pl.pallas_call(
        functools.partial(_idft1_body, seq_len=seq_len, pages=pages, cc=cc, dd=dd),
        grid=(b, PAGE // cc, d // dd),
        in_specs=[
            pl.BlockSpec(g_inv.shape, lambda i, j, k: (0, 0)),
            pl.BlockSpec((1, 2, pages, cc, dd), lambda i, j, k: (i, 0, 0, j, k)),
            pl.BlockSpec((1, pages, cc, dd), lambda i, j, k: (i, 0, j, k)),
            pl.BlockSpec((1, pages, cc, dd), lambda i, j, k: (i, 0, j, k)),
            pl.BlockSpec((1, dd), lambda i, j, k: (0, k)),
        ],
        out_specs=pl.BlockSpec((1, pages, cc, dd), lambda i, j, k: (i, 0, j, k)),
        out_shape=jax.ShapeDtypeStruct(z.shape, BF16),
        compiler_params=_cparams("parallel", "parallel", "parallel"),
        name="idft_stage1",
    )(g_inv, bint, z, gate, skip.astype(F32).reshape(1, d))


def _hyena_mixer(hn_proj, conv_w, conv_b, kern, skip, tabs, *, seq_len, tm, cc, dd, bb):
    b, lp, _ = hn_proj.shape
    d = D_MODEL
    pages = lp // PAGE
    x1, x2, v = (t.reshape(b, pages, PAGE, d) for t in _short_conv(hn_proj, conv_w, conv_b, tm=tm, seq_len=seq_len))
    af = _dft1(tabs["g_filt"], kern.reshape(HYENA_ORDER, tabs["n1"], PAGE, d), cc=cc, dd=dd)
    kf = _filter_spectrum(tabs["t_fwd"], af)
    z = v
    for o, gate in enumerate((x1, x2)):
        a = _dft1(tabs["g_data"], z, cc=cc, dd=dd)
        r = _spectral(tabs["t_fwd"], tabs["t_inv"], kf[o], a, bb=bb)
        z = _idft1(tabs["g_inv"], r, z, gate, skip[o], cc=cc, dd=dd, seq_len=seq_len)
    return z.reshape(b, lp, d)


def _t5_bucket(rel):
    half = REL_BUCKETS // 2
    max_exact = half // 2
    n = jnp.abs(rel)
    large = max_exact + (jnp.log(jnp.maximum(n, 1).astype(F32) / max_exact)
                         / math.log(REL_MAX_DIST / max_exact) * (half - max_exact)).astype(jnp.int32)
    large = jnp.minimum(large, half - 1)
    return jnp.where(rel > 0, half, 0) + jnp.where(n < max_exact, n, large)


def _bias_table_body(rb_ref, bucket_ref, o_ref):
    bk = bucket_ref[...]
    for h in range(N_HEADS):
        acc = jnp.where(bk < 0, NEG, 0.0).astype(F32)
        for r in range(REL_BUCKETS):
            acc = jnp.where(bk == r, rb_ref[r, h], acc)
        o_ref[h] = acc


def _bias_table(rel_bias, bucket, *, tr):
    r, c = bucket.shape
    return pl.pallas_call(
        _bias_table_body,
        grid=(r // tr,),
        in_specs=[
            pl.BlockSpec(memory_space=pltpu.SMEM),
            pl.BlockSpec((tr, c), lambda i: (i, 0)),
        ],
        out_specs=pl.BlockSpec((N_HEADS, tr, c), lambda i: (0, i, 0)),
        out_shape=jax.ShapeDtypeStruct((N_HEADS, r, c), F32),
        compiler_params=_cparams("parallel"),
        name="bias_table",
    )(rel_bias.astype(F32), bucket)


def _attn_body(sink_ref, q_ref, kp_ref, kc_ref, kn_ref, vp_ref, vc_ref, vn_ref, km_ref, vm_ref,
               bb_ref, mb_ref, o_ref, *, seq_len, nb):
    i = pl.program_id(1)
    hd = HEAD_DIM
    kpos = (i - 1) * BLOCK + lax.broadcasted_iota(jnp.int32, (1, 3 * BLOCK), 1)
    kvalid = (kpos >= N_META) & (kpos < seq_len)
    nk = 3 * BLOCK + N_META
    ones = jnp.ones((nk, hd), BF16)
    for g in range(N_KV_HEADS):
        ks = slice(g * hd, (g + 1) * hd)
        hs = slice(g * GROUP, (g + 1) * GROUP)
        k_all = jnp.concatenate([kp_ref[0, :, ks], kc_ref[0, :, ks], kn_ref[0, :, ks], km_ref[0, :, ks]], axis=0)
        v_all = jnp.concatenate([vp_ref[0, :, ks], vc_ref[0, :, ks], vn_ref[0, :, ks], vm_ref[0, :, ks]], axis=0)
        v_ext = jnp.concatenate([v_all, ones], axis=1)
        q = jnp.concatenate([q_ref[0, :, (g * GROUP + j) * hd:(g * GROUP + j + 1) * hd] for j in range(GROUP)],
                            axis=0)
        bias = jnp.concatenate(
            [jnp.where(kvalid, bb_ref[hs].reshape(GROUP * BLOCK, 3 * BLOCK), NEG),
             mb_ref[hs].reshape(GROUP * BLOCK, N_META)], axis=1)
        s = lax.dot_general(q, k_all, (((1,), (1,)), ((), ())), preferred_element_type=F32) + bias
        sink = jnp.concatenate([jnp.full((BLOCK, 1), sink_ref[g * GROUP + j], F32) for j in range(GROUP)], axis=0)
        m = jnp.maximum(jnp.max(s, axis=-1, keepdims=True), sink)
        p = jnp.exp(s - m).astype(BF16)
        r = jnp.dot(p, v_ext, preferred_element_type=F32)
        o = r[:, :hd] / (r[:, hd:] + jnp.exp(sink - m))
        for j in range(GROUP):
            h = g * GROUP + j
            o_ref[0, :, h * hd:(h + 1) * hd] = o[j * BLOCK:(j + 1) * BLOCK].astype(o_ref.dtype)


def _attention(qkv, sink, band_bias, meta_bias, *, seq_len):
    b, lp, _ = qkv.shape
    nb = lp // BLOCK
    dq = N_HEADS * HEAD_DIM
    dkv = N_KV_HEADS * HEAD_DIM
    kcol = dq // dkv
    vcol = kcol + 1
    prv = lambda i, j: jnp.maximum(j - 1, 0)
    nxt = lambda i, j: jnp.minimum(j + 1, nb - 1)
    blk = lambda rows, col: pl.BlockSpec((1, BLOCK, dkv), lambda i, j: (i, rows(i, j), col))
    cur = lambda i, j: j
    return pl.pallas_call(
        functools.partial(_attn_body, seq_len=seq_len, nb=nb),
        grid=(b, nb),
        in_specs=[
            pl.BlockSpec(memory_space=pltpu.SMEM),
            pl.BlockSpec((1, BLOCK, dq), lambda i, j: (i, j, 0)),
            blk(prv, kcol), blk(cur, kcol), blk(nxt, kcol),
            blk(prv, vcol), blk(cur, vcol), blk(nxt, vcol),
            pl.BlockSpec((1, N_META, dkv), lambda i, j: (i, 0, kcol)),
            pl.BlockSpec((1, N_META, dkv), lambda i, j: (i, 0, vcol)),
            pl.BlockSpec((N_HEADS, BLOCK, 3 * BLOCK), lambda i, j: (0, 0, 0)),
            pl.BlockSpec((N_HEADS, BLOCK, N_META), lambda i, j: (0, j, 0)),
        ],
        out_specs=pl.BlockSpec((1, BLOCK, dq), lambda i, j: (i, j, 0)),
        out_shape=jax.ShapeDtypeStruct((b, lp, dq), BF16),
        compiler_params=_cparams("parallel", "parallel"),
        name="window_attention",
    )(sink.astype(F32), qkv, qkv, qkv, qkv, qkv, qkv, qkv, qkv, qkv, band_bias, meta_bias)


def _attention_bias_tables(rel_bias, lp):
    r = jnp.arange(BLOCK, dtype=jnp.int32)[:, None]
    s = jnp.arange(3 * BLOCK, dtype=jnp.int32)[None, :]
    rel = s - BLOCK - r
    band_bucket = jnp.where(jnp.abs(rel) <= WINDOW, _t5_bucket(rel), -1)
    qpos = jnp.arange(lp, dtype=jnp.int32)[:, None]
    meta_bucket = _t5_bucket(jnp.arange(N_META, dtype=jnp.int32)[None, :] - qpos)
    return (_bias_table(rel_bias, band_bucket, tr=BLOCK), _bias_table(rel_bias, meta_bucket, tr=_seq_tile(lp)))


def _seq_tile(lp):
    pages = lp // PAGE
    for k in (5, 4, 3, 2, 1):
        if pages % k == 0:
            return k * PAGE
    return PAGE


def _trunk(x, meta_tokens, rel_bias, mix_norm, mlp_norm, hy, at, mlp_w_up, mlp_w_down):
    b, s, d = x.shape
    seq_len = N_META + s
    pages = -(-seq_len // PAGE)
    lp = pages * PAGE
    assert seq_len <= lp - 64
    tm_seq = _seq_tile(lp)
    tm_flat = 512
    assert (b * lp) % tm_flat == 0
    bb = 4

    meta = jnp.broadcast_to(meta_tokens[None].astype(x.dtype), (b, N_META, d))
    h = jnp.concatenate([meta, x, jnp.zeros((b, lp - seq_len, d), x.dtype)], axis=1)

    tabs = _dft_tables(pages)
    band_bias, meta_bias = _attention_bias_tables(rel_bias, lp)

    depth = mix_norm.shape[0]
    for i in range(depth):
        j = i // 2
        if i % 2 == 0:
            u = _norm_matmul(h, mix_norm[i], hy["w_in"][j], tm=tm_seq, seq_len=seq_len)
            kern = _hyena_kernels(seq_len, tabs["n"], hy["f_w1"][j], hy["f_b1"][j], hy["f_freq1"][j], hy["f_w2"][j],
                                  hy["f_b2"][j], hy["f_freq2"][j], hy["f_w3"][j])
            z = _hyena_mixer(u, hy["conv_w"][j], hy["conv_b"][j], kern, hy["skip"][j], tabs,
                             seq_len=seq_len, tm=tm_seq, cc=DFT_ROWS, dd=DFT_COLS, bb=bb)
            w_out = hy["w_out"][j]
        else:
            qkv = _qkv_proj(h, mix_norm[i], at["w_qkv"][j], at["q_norm"][j], at["k_norm"][j], tm=tm_seq)
            z = _attention(qkv, at["sink"][j], band_bias, meta_bias, seq_len=seq_len)
            w_out = at["w_out"][j]
        hf = _proj_residual(h.reshape(b * lp, d), z.reshape(b * lp, d), w_out, tm=tm_flat)
        hf = _mlp(hf, mlp_norm[i], mlp_w_up[i], mlp_w_down[i], tm=tm_flat)
        h = hf.reshape(b, lp, d)
    return h[:, N_META:seq_len]


def kernel(x_prompt, x_sample, meta_tokens, rel_bias, mix_norm, mlp_norm, hy_w_in, hy_conv_w, hy_conv_b, hy_f_w1, hy_f_b1, hy_f_freq1, hy_f_w2, hy_f_b2, hy_f_freq2, hy_f_w3, hy_skip, hy_w_out, at_w_qkv, at_q_norm, at_k_norm, at_sink, at_w_out, mlp_w_up, mlp_w_down):
    hy = dict(w_in=hy_w_in.astype(BF16), conv_w=hy_conv_w, conv_b=hy_conv_b, f_w1=hy_f_w1, f_b1=hy_f_b1,
              f_freq1=hy_f_freq1, f_w2=hy_f_w2, f_b2=hy_f_b2, f_freq2=hy_f_freq2, f_w3=hy_f_w3,
              skip=hy_skip, w_out=hy_w_out.astype(BF16))
    at = dict(w_qkv=at_w_qkv.astype(BF16), q_norm=at_q_norm, k_norm=at_k_norm, sink=at_sink,
              w_out=at_w_out.astype(BF16))
    w_up = mlp_w_up.astype(BF16)
    w_down = mlp_w_down.astype(BF16)
    run = functools.partial(_trunk, meta_tokens=meta_tokens, rel_bias=rel_bias, mix_norm=mix_norm,
                            mlp_norm=mlp_norm, hy=hy, at=at, mlp_w_up=w_up, mlp_w_down=w_down)
    return (run(x_prompt), run(x_sample))
```

```python
import functools
import math

import jax
import jax.numpy as jnp
from jax import lax
from jax.experimental import pallas as pl
from jax.experimental.pallas import tpu as pltpu

F32 = jnp.float32
BF16 = jnp.bfloat16

D_MODEL = 1024
N_META = 16
HYENA_ORDER = 2
FILTER_EMB = 33
FILTER_HIDDEN = 64
FILTER_BANDS = (FILTER_EMB - 1) // 2
DECAY_TARGET = 1e-2
FAST_DECAY_PCT = 0.3
SLOW_DECAY_PCT = 1.5
N_HEADS = 16
N_KV_HEADS = 4
HEAD_DIM = D_MODEL // N_HEADS
GROUP = N_HEADS // N_KV_HEADS
WINDOW = 128
BLOCK = 128
REL_BUCKETS = 32
REL_MAX_DIST = 128
D_FF = 4 * D_MODEL
EPS = 1e-6
NEG = -1e30

PAGE = 128
LANES = 128
FEAT_PAD = LANES
VMEM_LIMIT_BYTES = 52 * 1024 * 1024
SUB = 16
HALO = SUB
DFT_ROWS = 32
DFT_COLS = 512


def _cparams(*sem):
    return pltpu.CompilerParams(dimension_semantics=sem, vmem_limit_bytes=VMEM_LIMIT_BYTES)


def _rms(x, g):
    ms = jnp.mean(x * x, axis=-1, keepdims=True)
    return (x * lax.rsqrt(ms + EPS)) * g


def _norm_matmul_body(x_ref, g_ref, w_ref, o_ref, *, seq_len, tm):
    xn = _rms(x_ref[0], g_ref[...])
    y = jnp.dot(xn.astype(BF16), w_ref[...], preferred_element_type=F32)
    if seq_len is not None:
        row = pl.program_id(1) * tm + lax.broadcasted_iota(jnp.int32, (tm, 1), 0)
        y = jnp.where(row < seq_len, y, 0.0)
    o_ref[0] = y.astype(o_ref.dtype)


def _norm_matmul(h, g, w, *, tm, seq_len=None):
    b, lp, d = h.shape
    n = w.shape[1]
    return pl.pallas_call(
        functools.partial(_norm_matmul_body, seq_len=seq_len, tm=tm),
        grid=(b, lp // tm),
        in_specs=[
            pl.BlockSpec((1, tm, d), lambda i, j: (i, j, 0)),
            pl.BlockSpec((1, d), lambda i, j: (0, 0)),
            pl.BlockSpec((d, n), lambda i, j: (0, 0)),
        ],
        out_specs=pl.BlockSpec((1, tm, n), lambda i, j: (i, j, 0)),
        out_shape=jax.ShapeDtypeStruct((b, lp, n), BF16),
        compiler_params=_cparams("parallel", "parallel"),
        name="norm_matmul",
    )(h, g.reshape(1, d), w)


def _qkv_body(x_ref, g_ref, w_ref, gain_ref, o_ref):
    xn = _rms(x_ref[0], g_ref[...])
    y = jnp.dot(xn.astype(BF16), w_ref[...], preferred_element_type=F32)
    nqk = (N_HEADS + N_KV_HEADS) * HEAD_DIM
    low = lax.broadcasted_iota(jnp.int32, (1, LANES), 1) < HEAD_DIM
    for p in range(nqk // LANES):
        cols = slice(p * LANES, (p + 1) * LANES)
        y2 = y[:, cols]
        sq = y2 * y2
        ms_lo = jnp.sum(jnp.where(low, sq, 0.0), axis=-1, keepdims=True) * (1.0 / HEAD_DIM)
        ms_hi = jnp.sum(jnp.where(low, 0.0, sq), axis=-1, keepdims=True) * (1.0 / HEAD_DIM)
        r = jnp.where(low, lax.rsqrt(ms_lo + EPS), lax.rsqrt(ms_hi + EPS))
        o_ref[0, :, cols] = (y2 * r * gain_ref[:, cols]).astype(o_ref.dtype)
    o_ref[0, :, nqk:] = y[:, nqk:].astype(o_ref.dtype)


def _qkv_proj(h, g, w, q_gain, k_gain, *, tm):
    b, lp, d = h.shape
    n = w.shape[1]
    gain = jnp.concatenate([jnp.tile(q_gain.astype(F32) * (HEAD_DIM ** -0.5), N_HEADS),
                            jnp.tile(k_gain.astype(F32), N_KV_HEADS)])[None, :]
    return pl.pallas_call(
        _qkv_body,
        grid=(b, lp // tm),
        in_specs=[
            pl.BlockSpec((1, tm, d), lambda i, j: (i, j, 0)),
            pl.BlockSpec((1, d), lambda i, j: (0, 0)),
            pl.BlockSpec((d, n), lambda i, j: (0, 0)),
            pl.BlockSpec(gain.shape, lambda i, j: (0, 0)),
        ],
        out_specs=pl.BlockSpec((1, tm, n), lambda i, j: (i, j, 0)),
        out_shape=jax.ShapeDtypeStruct((b, lp, n), BF16),
        compiler_params=_cparams("parallel", "parallel"),
        name="qkv_proj",
    )(h, g.reshape(1, d), w, gain)


def _mixer_out_mlp_body(x_ref, z_ref, wo_ref, g_ref, wu_ref, wd_ref, o_ref, *, f_chunk):
    x = x_ref[...] + jnp.dot(z_ref[...], wo_ref[...], preferred_element_type=F32)
    xb = _rms(x, g_ref[...]).astype(BF16)
    acc = x
    for f in range(D_FF // f_chunk):
        u = jnp.dot(xb, wu_ref[:, f * f_chunk:(f + 1) * f_chunk], preferred_element_type=F32)
        a = jnp.square(jnp.maximum(u, 0.0)).astype(BF16)
        acc = acc + jnp.dot(a, wd_ref[f * f_chunk:(f + 1) * f_chunk, :], preferred_element_type=F32)
    o_ref[...] = acc


def _mixer_out_mlp(h, z, w_out, g, w_up, w_down, *, tm, f_chunk=1024):
    m, d = h.shape
    k = z.shape[1]
    return pl.pallas_call(
        functools.partial(_mixer_out_mlp_body, f_chunk=f_chunk),
        grid=(m // tm,),
        in_specs=[
            pl.BlockSpec((tm, d), lambda i: (i, 0)),
            pl.BlockSpec((tm, k), lambda i: (i, 0)),
            pl.BlockSpec((k, d), lambda i: (0, 0)),
            pl.BlockSpec((1, d), lambda i: (0, 0)),
            pl.BlockSpec((d, D_FF), lambda i: (0, 0)),
            pl.BlockSpec((D_FF, d), lambda i: (0, 0)),
        ],
        out_specs=pl.BlockSpec((tm, d), lambda i: (i, 0)),
        out_shape=jax.ShapeDtypeStruct((m, d), F32),
        compiler_params=_cparams("parallel"),
        name="mixer_out_mlp",
    )(h, z, w_out, g.reshape(1, d), w_up, w_down)


def _short_conv_body(prev_ref, cur_ref, next_ref, w_ref, b_ref, o_ref, *, seq_len, tm):
    j = pl.program_id(1)
    x = cur_ref[0].astype(F32)
    prev_row = jnp.where(j > 0, prev_ref[0].astype(F32)[HALO - 1:HALO, :], 0.0)
    next_row = next_ref[0].astype(F32)[0:1, :]
    rid = lax.broadcasted_iota(jnp.int32, (tm, 1), 0)
    up = jnp.where(rid == 0, prev_row, pltpu.roll(x, 1, 0))
    dn = jnp.where(rid == tm - 1, next_row, pltpu.roll(x, tm - 1, 0))
    y = up * w_ref[0, 0:1, :] + x * w_ref[0, 1:2, :] + dn * w_ref[0, 2:3, :] + b_ref[0]
    y = jnp.where(j * tm + rid < seq_len, y, 0.0)
    o_ref[0, 0] = y.astype(o_ref.dtype)


def _short_conv(u, w, bias, *, tm, seq_len):
    b, lp, d3 = u.shape
    d = d3 // 3
    nh = tm // HALO
    last = lp // HALO - 1
    w3 = w.reshape(3, 3, d).transpose(1, 0, 2)
    b3 = bias.reshape(3, 1, d)
    return pl.pallas_call(
        functools.partial(_short_conv_body, seq_len=seq_len, tm=tm),
        grid=(b, lp // tm, 3),
        in_specs=[
            pl.BlockSpec((1, HALO, d), lambda i, j, c: (i, jnp.maximum(j * nh - 1, 0), c)),
            pl.BlockSpec((1, tm, d), lambda i, j, c: (i, j, c)),
            pl.BlockSpec((1, HALO, d), lambda i, j, c: (i, jnp.minimum((j + 1) * nh, last), c)),
            pl.BlockSpec((1, 3, d), lambda i, j, c: (c, 0, 0)),
            pl.BlockSpec((1, 1, d), lambda i, j, c: (c, 0, 0)),
        ],
        out_specs=pl.BlockSpec((1, 1, tm, d), lambda i, j, c: (c, i, j, 0)),
        out_shape=jax.ShapeDtypeStruct((3, b, lp, d), BF16),
        compiler_params=_cparams("parallel", "parallel", "parallel"),
        name="short_conv",
    )(u, u, u, w3, b3)


def _filter_body(z_ref, w1_ref, b1_ref, fr1_ref, w2_ref, b2_ref, fr2_ref, w3_ref, delta_ref, o_ref,
                 *, seq_len, n, tm):
    hp = lax.Precision.HIGHEST
    z = z_ref[...]
    h = jnp.sin(fr1_ref[...] * (jnp.dot(z, w1_ref[...], precision=hp, preferred_element_type=F32) + b1_ref[...]))
    h = jnp.sin(fr2_ref[...] * (jnp.dot(h, w2_ref[...], precision=hp, preferred_element_type=F32) + b2_ref[...]))
    k = jnp.dot(h, w3_ref[...], precision=hp, preferred_element_type=F32)
    decay = jnp.exp(-z[:, 0:1] * delta_ref[...])
    d = delta_ref.shape[1]
    p = pl.program_id(0) * tm + lax.broadcasted_iota(jnp.int32, (tm, 1), 0)
    fwd = p < seq_len
    bwd = p > n - seq_len
    for o in range(HYENA_ORDER):
        kf = k[:, (2 * o) * d:(2 * o + 1) * d]
        kb = k[:, (2 * o + 1) * d:(2 * o + 2) * d]
        o_ref[o] = (jnp.where(fwd, kf, jnp.where(bwd, kb, 0.0)) * decay).astype(o_ref.dtype)


def _pad2(x, rows, cols):
    return jnp.pad(x, ((0, rows - x.shape[0]), (0, cols - x.shape[1])))


def _hyena_kernels(seq_len, n, f_w1, f_b1, f_freq1, f_w2, f_b2, f_freq2, f_w3):
    d = D_MODEL
    tm = _seq_tile(n)
    p = jnp.arange(n, dtype=jnp.int32)
    lag = jnp.where(p < seq_len, p, jnp.where(p > n - seq_len, n - p, 0))
    t = jnp.linspace(0.0, 1.0, seq_len, dtype=F32)[lag][:, None]
    w = ((2.0 * math.pi / seq_len) * jnp.arange(seq_len, dtype=F32))[lag][:, None]
    f = jnp.linspace(1e-4, FILTER_BANDS - 1, FILTER_BANDS, dtype=F32)[None, :]
    z = _pad2(jnp.concatenate([t, jnp.cos(f * w), -jnp.sin(f * w)], axis=-1), n, FEAT_PAD)
    max_decay = math.log(DECAY_TARGET) / FAST_DECAY_PCT
    min_decay = math.log(DECAY_TARGET) / SLOW_DECAY_PCT
    deltas = jnp.abs(jnp.linspace(min_decay, max_decay, d, dtype=F32))[None, :]
    hpad = FEAT_PAD
    args = (
        z,
        _pad2(f_w1, FEAT_PAD, hpad), _pad2(f_b1[None, :], 1, hpad), _pad2(f_freq1[None, :], 1, hpad),
        _pad2(f_w2, hpad, hpad), _pad2(f_b2[None, :], 1, hpad), _pad2(f_freq2[None, :], 1, hpad),
        _pad2(f_w3, hpad, f_w3.shape[1]), deltas,
    )
    const = lambda i: (0, 0)
    return pl.pallas_call(
        functools.partial(_filter_body, seq_len=seq_len, n=n, tm=tm),
        grid=(n // tm,),
        in_specs=[pl.BlockSpec((tm, FEAT_PAD), lambda i: (i, 0))]
        + [pl.BlockSpec(a.shape, const) for a in args[1:]],
        out_specs=pl.BlockSpec((HYENA_ORDER, tm, d), lambda i: (0, i, 0)),
        out_shape=jax.ShapeDtypeStruct((HYENA_ORDER, n, d), BF16),
        compiler_params=_cparams("parallel"),
        name="hyena_filter",
    )(*args)


def _angle(num, den):
    return (2.0 * math.pi / den) * (num % den).astype(F32)


def _dft_tables(pages):
    n1 = 2 * pages - 1
    n = n1 * PAGE
    k1 = jnp.arange(pages, dtype=jnp.int32)
    eye = jnp.eye(SUB, dtype=F32)

    def stage1(n_in):
        a = jnp.arange(n_in, dtype=jnp.int32)
        ang = _angle(k1[:, None] * a[None, :], n1)
        return jnp.kron(jnp.concatenate([jnp.cos(ang), -jnp.sin(ang)], axis=0), eye).astype(BF16)
    g_data = stage1(pages)
    g_filt = stage1(n1)
    c = jnp.arange(PAGE, dtype=jnp.int32)
    k = k1[:, None, None] + n1 * c[None, :, None]
    ang = _angle(k * c[None, None, :], n)
    cs, sn = jnp.cos(ang), jnp.sin(ang)
    t_fwd = jnp.concatenate([jnp.concatenate([cs, sn], axis=2),
                             jnp.concatenate([-sn, cs], axis=2)], axis=1).astype(BF16)
    cst, snt = cs.transpose(0, 2, 1), sn.transpose(0, 2, 1)
    t_inv = jnp.concatenate([jnp.concatenate([cst, -snt], axis=2),
                             jnp.concatenate([snt, cst], axis=2)], axis=1).astype(BF16)
    a = jnp.arange(pages, dtype=jnp.int32)
    ang = _angle(a[:, None] * k1[None, :], n1)
    coef = jnp.where(k1 == 0, 1.0, 2.0)[None, :] / n
    g_inv = jnp.kron(jnp.concatenate([coef * jnp.cos(ang), -coef * jnp.sin(ang)], axis=1), eye).astype(BF16)
    return dict(g_data=g_data, g_filt=g_filt, t_fwd=t_fwd, t_inv=t_inv, g_inv=g_inv, n=n, n1=n1)


def _dft1_body(g_ref, z_ref, a_ref, *, pin, pout, cc, dd):
    for j in range(cc // SUB):
        rows = slice(j * SUB, (j + 1) * SUB)
        x = z_ref[0, :, rows, :].reshape(pin * SUB, dd)
        r = jnp.dot(g_ref[...], x, preferred_element_type=F32)
        a_ref[0, :, :, rows, :] = r.reshape(2, pout, SUB, dd).astype(a_ref.dtype)


def _dft1(g, z, *, cc, dd):
    b, pin, _, d = z.shape
    pout = g.shape[0] // (2 * SUB)
    return pl.pallas_call(
        functools.partial(_dft1_body, pin=pin, pout=pout, cc=cc, dd=dd),
        grid=(b, PAGE // cc, d // dd),
        in_specs=[
            pl.BlockSpec(g.shape, lambda i, j, k: (0, 0)),
            pl.BlockSpec((1, pin, cc, dd), lambda i, j, k: (i, 0, j, k)),
        ],
        out_specs=pl.BlockSpec((1, 2, pout, cc, dd), lambda i, j, k: (i, 0, 0, j, k)),
        out_shape=jax.ShapeDtypeStruct((b, 2, pout, PAGE, d), BF16),
        compiler_params=_cparams("parallel", "parallel", "parallel"),
        name="dft_stage1",
    )(g, z)


def _filter_spectrum_body(t_ref, a_ref, o_ref):
    a = jnp.concatenate([a_ref[0, 0, 0], a_ref[0, 1, 0]], axis=0)
    x = jnp.dot(t_ref[0], a, preferred_element_type=F32)
    o_ref[0, 0, 0] = x[:PAGE]
    o_ref[0, 1, 0] = x[PAGE:]


def _filter_spectrum(t_fwd, af):
    o, _, pages, _, d = af.shape
    return pl.pallas_call(
        _filter_spectrum_body,
        grid=(pages, o),
        in_specs=[
            pl.BlockSpec((1, 2 * PAGE, 2 * PAGE), lambda k, i: (k, 0, 0)),
            pl.BlockSpec((1, 2, 1, PAGE, d), lambda k, i: (i, 0, k, 0, 0)),
        ],
        out_specs=pl.BlockSpec((1, 2, 1, PAGE, d), lambda k, i: (i, 0, k, 0, 0)),
        out_shape=jax.ShapeDtypeStruct(af.shape, F32),
        compiler_params=_cparams("parallel", "parallel"),
        name="filter_spectrum",
    )(t_fwd, af)


def _spectral_body(t_ref, ti_ref, kf_ref, a_ref, o_ref, *, bb):
    kr = kf_ref[0, 0]
    ki = kf_ref[1, 0]
    for b in range(bb):
        a = jnp.concatenate([a_ref[b, 0, 0], a_ref[b, 1, 0]], axis=0)
        x = jnp.dot(t_ref[0], a, preferred_element_type=F32)
        xr, xi = x[:PAGE], x[PAGE:]
        y = jnp.concatenate([xr * kr - xi * ki, xr * ki + xi * kr], axis=0).astype(BF16)
        r = jnp.dot(ti_ref[0], y, preferred_element_type=F32)
        o_ref[b, 0, 0] = r[:PAGE].astype(o_ref.dtype)
        o_ref[b, 1, 0] = r[PAGE:].astype(o_ref.dtype)


def _spectral(t_fwd, t_inv, kf, a5, *, bb):
    b, _, pages, _, d = a5.shape
    return pl.pallas_call(
        functools.partial(_spectral_body, bb=bb),
        grid=(pages, b // bb),
        in_specs=[
            pl.BlockSpec((1, 2 * PAGE, 2 * PAGE), lambda k, i: (k, 0, 0)),
            pl.BlockSpec((1, 2 * PAGE, 2 * PAGE), lambda k, i: (k, 0, 0)),
            pl.BlockSpec((2, 1, PAGE, d), lambda k, i: (0, k, 0, 0)),
            pl.BlockSpec((bb, 2, 1, PAGE, d), lambda k, i: (i, 0, k, 0, 0)),
        ],
        out_specs=pl.BlockSpec((bb, 2, 1, PAGE, d), lambda k, i: (i, 0, k, 0, 0)),
        out_shape=jax.ShapeDtypeStruct(a5.shape, BF16),
        compiler_params=_cparams("parallel", "parallel"),
        name="spectral_stage2",
    )(t_fwd, t_inv, kf, a5)


def _idft1_body(gi_ref, b_ref, z_ref, gate_ref, skip_ref, o_ref, *, seq_len, pages, cc, dd):
    base = (lax.broadcasted_iota(jnp.int32, (pages, SUB, 1), 0) * PAGE + pl.program_id(1) * cc
            + lax.broadcasted_iota(jnp.int32, (pages, SUB, 1), 1))
    for j in range(cc // SUB):
        rows = slice(j * SUB, (j + 1) * SUB)
        x = b_ref[0, :, :, rows, :].reshape(2 * pages * SUB, dd)
        y = jnp.dot(gi_ref[...], x, preferred_element_type=F32).reshape(pages, SUB, dd)
        out = gate_ref[0, :, rows, :].astype(F32) * (y + skip_ref[...] * z_ref[0, :, rows, :].astype(F32))
        o_ref[0, :, rows, :] = jnp.where(base + j * SUB < seq_len, out, 0.0).astype(o_ref.dtype)


def _idft1(g_inv, bint, z, gate, skip, *, cc, dd, seq_len):
    b, pages, _, d = z.shape
    return pl.pallas_call(
        functools.partial(_idft1_body, seq_len=seq_len, pages=pages, cc=cc, dd=dd),
        grid=(b, PAGE // cc, d // dd),
        in_specs=[
            pl.BlockSpec(g_inv.shape, lambda i, j, k: (0, 0)),
            pl.BlockSpec((1, 2, pages, cc, dd), lambda i, j, k: (i, 0, 0, j, k)),
            pl.BlockSpec((1, pages, cc, dd), lambda i, j, k: (i, 0, j, k)),
            pl.BlockSpec((1, pages, cc, dd), lambda i, j, k: (i, 0, j, k)),
            pl.BlockSpec((1, dd), lambda i, j, k: (0, k)),
        ],
        out_specs=pl.BlockSpec((1, pages, cc, dd), lambda i, j, k: (i, 0, j, k)),
        out_shape=jax.ShapeDtypeStruct(z.shape, BF16),
        compiler_params=_cparams("parallel", "parallel", "parallel"),
        name="idft_stage1",
    )(g_inv, bint, z, gate, skip.astype(F32).reshape(1, d))


def _hyena_mixer(hn_proj, conv_w, conv_b, kern, skip, tabs, *, seq_len, tm, cc, dd, bb):
    b, lp, _ = hn_proj.shape
    d = D_MODEL
    pages = lp // PAGE
    x1, x2, v = (t.reshape(b, pages, PAGE, d) for t in _short_conv(hn_proj, conv_w, conv_b, tm=tm, seq_len=seq_len))
    af = _dft1(tabs["g_filt"], kern.reshape(HYENA_ORDER, tabs["n1"], PAGE, d), cc=cc, dd=dd)
    kf = _filter_spectrum(tabs["t_fwd"], af)
    z = v
    for o, gate in enumerate((x1, x2)):
        a = _dft1(tabs["g_data"], z, cc=cc, dd=dd)
        r = _spectral(tabs["t_fwd"], tabs["t_inv"], kf[o], a, bb=bb)
        z = _idft1(tabs["g_inv"], r, z, gate, skip[o], cc=cc, dd=dd, seq_len=seq_len)
    return z.reshape(b, lp, d)


def _t5_bucket(rel):
    half = REL_BUCKETS // 2
    max_exact = half // 2
    n = jnp.abs(rel)
    large = max_exact + (jnp.log(jnp.maximum(n, 1).astype(F32) / max_exact)
                         / math.log(REL_MAX_DIST / max_exact) * (half - max_exact)).astype(jnp.int32)
    large = jnp.minimum(large, half - 1)
    return jnp.where(rel > 0, half, 0) + jnp.where(n < max_exact, n, large)


def _bias_table_body(rb_ref, bucket_ref, o_ref):
    bk = bucket_ref[...]
    for h in range(N_HEADS):
        acc = jnp.where(bk < 0, NEG, 0.0).astype(F32)
        for r in range(REL_BUCKETS):
            acc = jnp.where(bk == r, rb_ref[r, h], acc)
        o_ref[h] = acc


def _bias_table(rel_bias, bucket, *, tr):
    r, c = bucket.shape
    return pl.pallas_call(
        _bias_table_body,
        grid=(r // tr,),
        in_specs=[
            pl.BlockSpec(memory_space=pltpu.SMEM),
            pl.BlockSpec((tr, c), lambda i: (i, 0)),
        ],
        out_specs=pl.BlockSpec((N_HEADS, tr, c), lambda i: (0, i, 0)),
        out_shape=jax.ShapeDtypeStruct((N_HEADS, r, c), F32),
        compiler_params=_cparams("parallel"),
        name="bias_table",
    )(rel_bias.astype(F32), bucket)


def _attn_body(sink_ref, q_ref, kp_ref, kc_ref, kn_ref, vp_ref, vc_ref, vn_ref, km_ref, vm_ref,
               bb_ref, mb_ref, o_ref, *, seq_len, nb):
    i = pl.program_id(1)
    hd = HEAD_DIM
    kpos = (i - 1) * BLOCK + lax.broadcasted_iota(jnp.int32, (1, 3 * BLOCK), 1)
    kvalid = (kpos >= N_META) & (kpos < seq_len)
    nk = 3 * BLOCK + N_META
    ones = jnp.ones((nk, hd), BF16)
    for g in range(N_KV_HEADS):
        ks = slice(g * hd, (g + 1) * hd)
        hs = slice(g * GROUP, (g + 1) * GROUP)
        k_all = jnp.concatenate([kp_ref[0, :, ks], kc_ref[0, :, ks], kn_ref[0, :, ks], km_ref[0, :, ks]], axis=0)
        v_all = jnp.concatenate([vp_ref[0, :, ks], vc_ref[0, :, ks], vn_ref[0, :, ks], vm_ref[0, :, ks]], axis=0)
        v_ext = jnp.concatenate([v_all, ones], axis=1)
        q = jnp.concatenate([q_ref[0, :, (g * GROUP + j) * hd:(g * GROUP + j + 1) * hd] for j in range(GROUP)],
                            axis=0)
        bias = jnp.concatenate(
            [jnp.where(kvalid, bb_ref[hs].reshape(GROUP * BLOCK, 3 * BLOCK), NEG),
             mb_ref[hs].reshape(GROUP * BLOCK, N_META)], axis=1)
        s = lax.dot_general(q, k_all, (((1,), (1,)), ((), ())), preferred_element_type=F32) + bias
        sink = jnp.concatenate([jnp.full((BLOCK, 1), sink_ref[g * GROUP + j], F32) for j in range(GROUP)], axis=0)
        m = jnp.maximum(jnp.max(s, axis=-1, keepdims=True), sink)
        p = jnp.exp(s - m).astype(BF16)
        r = jnp.dot(p, v_ext, preferred_element_type=F32)
        o = r[:, :hd] / (r[:, hd:] + jnp.exp(sink - m))
        for j in range(GROUP):
            h = g * GROUP + j
            o_ref[0, :, h * hd:(h + 1) * hd] = o[j * BLOCK:(j + 1) * BLOCK].astype(o_ref.dtype)


def _attention(qkv, sink, band_bias, meta_bias, *, seq_len):
    b, lp, _ = qkv.shape
    nb = lp // BLOCK
    dq = N_HEADS * HEAD_DIM
    dkv = N_KV_HEADS * HEAD_DIM
    kcol = dq // dkv
    vcol = kcol + 1
    prv = lambda i, j: jnp.maximum(j - 1, 0)
    nxt = lambda i, j: jnp.minimum(j + 1, nb - 1)
    blk = lambda rows, col: pl.BlockSpec((1, BLOCK, dkv), lambda i, j: (i, rows(i, j), col))
    cur = lambda i, j: j
    return pl.pallas_call(
        functools.partial(_attn_body, seq_len=seq_len, nb=nb),
        grid=(b, nb),
        in_specs=[
            pl.BlockSpec(memory_space=pltpu.SMEM),
            pl.BlockSpec((1, BLOCK, dq), lambda i, j: (i, j, 0)),
            blk(prv, kcol), blk(cur, kcol), blk(nxt, kcol),
            blk(prv, vcol), blk(cur, vcol), blk(nxt, vcol),
            pl.BlockSpec((1, N_META, dkv), lambda i, j: (i, 0, kcol)),
            pl.BlockSpec((1, N_META, dkv), lambda i, j: (i, 0, vcol)),
            pl.BlockSpec((N_HEADS, BLOCK, 3 * BLOCK), lambda i, j: (0, 0, 0)),
            pl.BlockSpec((N_HEADS, BLOCK, N_META), lambda i, j: (0, j, 0)),
        ],
        out_specs=pl.BlockSpec((1, BLOCK, dq), lambda i, j: (i, j, 0)),
        out_shape=jax.ShapeDtypeStruct((b, lp, dq), BF16),
        compiler_params=_cparams("parallel", "parallel"),
        name="window_attention",
    )(sink.astype(F32), qkv, qkv, qkv, qkv, qkv, qkv, qkv, qkv, qkv, band_bias, meta_bias)


def _attention_bias_tables(rel_bias, lp):
    r = jnp.arange(BLOCK, dtype=jnp.int32)[:, None]
    s = jnp.arange(3 * BLOCK, dtype=jnp.int32)[None, :]
    rel = s - BLOCK - r
    band_bucket = jnp.where(jnp.abs(rel) <= WINDOW, _t5_bucket(rel), -1)
    qpos = jnp.arange(lp, dtype=jnp.int32)[:, None]
    meta_bucket = _t5_bucket(jnp.arange(N_META, dtype=jnp.int32)[None, :] - qpos)
    return (_bias_table(rel_bias, band_bucket, tr=BLOCK), _bias_table(rel_bias, meta_bucket, tr=_seq_tile(lp)))


def _seq_tile(lp):
    pages = lp // PAGE
    for k in (5, 4, 3, 2, 1):
        if pages % k == 0:
            return k * PAGE
    return PAGE


def _trunk(x, meta_tokens, rel_bias, mix_norm, mlp_norm, hy, at, mlp_w_up, mlp_w_down):
    b, s, d = x.shape
    seq_len = N_META + s
    pages = -(-seq_len // PAGE)
    lp = pages * PAGE
    assert seq_len <= lp - 64
    tm_seq = _seq_tile(lp)
    tm_flat = 512
    assert (b * lp) % tm_flat == 0
    bb = 4

    meta = jnp.broadcast_to(meta_tokens[None].astype(x.dtype), (b, N_META, d))
    h = jnp.concatenate([meta, x, jnp.zeros((b, lp - seq_len, d), x.dtype)], axis=1)

    tabs = _dft_tables(pages)
    band_bias, meta_bias = _attention_bias_tables(rel_bias, lp)

    depth = mix_norm.shape[0]
    for i in range(depth):
        j = i // 2
        if i % 2 == 0:
            u = _norm_matmul(h, mix_norm[i], hy["w_in"][j], tm=tm_seq, seq_len=seq_len)
            kern = _hyena_kernels(seq_len, tabs["n"], hy["f_w1"][j], hy["f_b1"][j], hy["f_freq1"][j], hy["f_w2"][j],
                                  hy["f_b2"][j], hy["f_freq2"][j], hy["f_w3"][j])
            z = _hyena_mixer(u, hy["conv_w"][j], hy["conv_b"][j], kern, hy["skip"][j], tabs,
                             seq_len=seq_len, tm=tm_seq, cc=DFT_ROWS, dd=DFT_COLS, bb=bb)
            w_out = hy["w_out"][j]
        else:
            qkv = _qkv_proj(h, mix_norm[i], at["w_qkv"][j], at["q_norm"][j], at["k_norm"][j], tm=tm_seq)
            z = _attention(qkv, at["sink"][j], band_bias, meta_bias, seq_len=seq_len)
            w_out = at["w_out"][j]
        hf = _mixer_out_mlp(h.reshape(b * lp, d), z.reshape(b * lp, d), w_out, mlp_norm[i], mlp_w_up[i],
                            mlp_w_down[i], tm=tm_flat)
        h = hf.reshape(b, lp, d)
    return h[:, N_META:seq_len]


def kernel(x_prompt, x_sample, meta_tokens, rel_bias, mix_norm, mlp_norm, hy_w_in, hy_conv_w, hy_conv_b, hy_f_w1, hy_f_b1, hy_f_freq1, hy_f_w2, hy_f_b2, hy_f_freq2, hy_f_w3, hy_skip, hy_w_out, at_w_qkv, at_q_norm, at_k_norm, at_sink, at_w_out, mlp_w_up, mlp_w_down):
    hy = dict(w_in=hy_w_in.astype(BF16), conv_w=hy_conv_w, conv_b=hy_conv_b, f_w1=hy_f_w1, f_b1=hy_f_b1,
              f_freq1=hy_f_freq1, f_w2=hy_f_w2, f_b2=hy_f_b2, f_freq2=hy_f_freq2, f_w3=hy_f_w3,
              skip=hy_skip, w_out=hy_w_out.astype(BF16))
    at = dict(w_qkv=at_w_qkv.astype(BF16), q_norm=at_q_norm, k_norm=at_k_norm, sink=at_sink,
              w_out=at_w_out.astype(BF16))
    w_up = mlp_w_up.astype(BF16)
    w_down = mlp_w_down.astype(BF16)
    run = functools.partial(_trunk, meta_tokens=meta_tokens, rel_bias=rel_bias, mix_norm=mix_norm,
                            mlp_norm=mlp_norm, hy=hy, at=at, mlp_w_up=w_up, mlp_w_down=w_down)
    return (run(x_prompt), run(x_sample))
```
